```python
import jax, jax.numpy as jnp
from jax import lax
import numpy as np

D_MODEL = 1024
BATCH = 4
SEQ = 4096
DEPTH = 2

GRID_W = 64
CTX_LEN = 256
NA_HEADS = 8
NA_HEAD_DIM = 64
NA_WIDTH = NA_HEADS * NA_HEAD_DIM
NA_KR_MAX = 8
NA_KC = 16
SC_WIDTH = D_MODEL - NA_WIDTH
SC_CONV_W = 3
IN_WIDTH = 3 * NA_WIDTH + 3 * SC_WIDTH
CONF_CONV_W = 31
N_EXPERTS = 16
N_GROUPS = 4
EXPERTS_PER_GROUP = N_EXPERTS // N_GROUPS
TOP_K = 2
EXPERT_FF = 512

N_EVEN = (DEPTH + 1) // 2
N_ODD = DEPTH // 2
EPS = 1e-6

kernel_name = "hybrid_natten_shortconv_conformer_grouped_moe_dit"


def rms_norm(x, g):
    xf = x.astype(jnp.float32)
    y = xf * lax.rsqrt(jnp.mean(xf * xf, axis=-1, keepdims=True) + EPS)
    return (y * g.astype(jnp.float32)).astype(x.dtype)


def layer_norm(x, g, b):
    xf = x.astype(jnp.float32)
    mu = jnp.mean(xf, axis=-1, keepdims=True)
    var = jnp.mean(jnp.square(xf - mu), axis=-1, keepdims=True)
    y = (xf - mu) * lax.rsqrt(var + EPS) * g.astype(jnp.float32) + b.astype(jnp.float32)
    return y.astype(x.dtype)


def modulate(h, shift, scale):
    return h * (1 + scale) + shift


def depthwise_conv(x, w):
    k, ch = w.shape
    return lax.conv_general_dilated(
        x, w[:, None, :].astype(x.dtype), window_strides=(1,),
        padding=[((k - 1) // 2, k // 2)],
        dimension_numbers=("NWC", "WIO", "NWC"), feature_group_count=ch)


def split_heads(t):
    return t.reshape(t.shape[0], t.shape[1], NA_HEADS, NA_HEAD_DIM)


def neighbourhood_attention(q, k, v, k_ctx, v_ctx, rpb):
    b, s, h, dh = q.shape
    rows = s // GRID_W
    kr = min(NA_KR_MAX, rows)
    scale = dh ** -0.5
    qg = q.reshape(b, rows, GRID_W, h, dh)
    kg = k.reshape(b, rows, GRID_W, h, dh)
    vg = v.reshape(b, rows, GRID_W, h, dh)
    cols = np.arange(GRID_W)
    col_start = np.clip(cols - NA_KC // 2, 0, GRID_W - NA_KC)
    col_idx = col_start[:, None] + np.arange(NA_KC)[None, :]
    dc = col_idx - cols[:, None] + (NA_KC - 1)
    rpb_c = rpb.astype(jnp.float32)[:, :, dc]

    def row_block(r):
        rs = jnp.clip(r - kr // 2, 0, rows - kr)
        q_r = lax.dynamic_index_in_dim(qg, r, axis=1, keepdims=False)
        k_rows = lax.dynamic_slice_in_dim(kg, rs, kr, axis=1)
        v_rows = lax.dynamic_slice_in_dim(vg, rs, kr, axis=1)
        k_nb = k_rows[:, :, col_idx]
        v_nb = v_rows[:, :, col_idx]
        dr = rs + jnp.arange(kr) - r + (NA_KR_MAX - 1)
        bias = jnp.transpose(rpb_c[:, dr], (0, 2, 1, 3))
        s_loc = jnp.einsum("bwhd,brwkhd->bhwrk", q_r, k_nb).astype(jnp.float32) * scale + bias[None]
        s_loc = s_loc.reshape(b, h, GRID_W, kr * NA_KC)
        s_ctx = jnp.einsum("bwhd,blhd->bhwl", q_r, k_ctx).astype(jnp.float32) * scale
        p = jax.nn.softmax(jnp.concatenate([s_loc, s_ctx], axis=-1), axis=-1).astype(v.dtype)
        p_loc = p[..., :kr * NA_KC].reshape(b, h, GRID_W, kr, NA_KC)
        p_ctx = p[..., kr * NA_KC:]
        return (jnp.einsum("bhwrk,brwkhd->bwhd", p_loc, v_nb)
                + jnp.einsum("bhwl,blhd->bwhd", p_ctx, v_ctx))

    out = lax.map(row_block, jnp.arange(rows))
    return jnp.transpose(out, (1, 0, 2, 3, 4)).reshape(b, s, h * dh)


def context_attention(q, k, v):
    b, l, h, dh = q.shape
    s = jnp.einsum("blhd,bmhd->bhlm", q, k).astype(jnp.float32) * dh ** -0.5
    p = jax.nn.softmax(s, axis=-1).astype(v.dtype)
    return jnp.einsum("bhlm,bmhd->blhd", p, v).reshape(b, l, h * dh)


SPLITS = [NA_WIDTH, 2 * NA_WIDTH, 3 * NA_WIDTH, 3 * NA_WIDTH + SC_WIDTH, 3 * NA_WIDTH + 2 * SC_WIDTH]


def even_mixer(hx, hc, w_in, rpb, conv_w, w_out, ctx_live):
    q, k, v, gb, gc, hb = jnp.split(hx @ w_in, SPLITS, axis=-1)
    if ctx_live:
        qc, k_c, v_c, gbc, gcc, hbc = jnp.split(hc @ w_in, SPLITS, axis=-1)
    else:
        k_c, v_c = jnp.split(hc @ w_in[:, NA_WIDTH:3 * NA_WIDTH], 2, axis=-1)
    k_c, v_c = split_heads(k_c), split_heads(v_c)
    a_x = neighbourhood_attention(split_heads(q), split_heads(k), split_heads(v), k_c, v_c, rpb)
    b_x = gb * depthwise_conv(gc * hb, conv_w)
    y_x = jnp.concatenate([a_x, b_x], axis=-1) @ w_out
    if not ctx_live:
        return y_x, None
    a_c = context_attention(split_heads(qc), k_c, v_c)
    b_c = gbc * depthwise_conv(gcc * hbc, conv_w)
    y_c = jnp.concatenate([a_c, b_c], axis=-1) @ w_out
    return y_x, y_c


def conformer_conv(h, pw1_w, pw1_b, dw_w, dw_b, ln_g, ln_b, pw2_w, pw2_b):
    a, g = jnp.split(h @ pw1_w + pw1_b, 2, axis=-1)
    u = a * jax.nn.sigmoid(g)
    u = depthwise_conv(u, dw_w) + dw_b
    u = jax.nn.silu(layer_norm(u, ln_g, ln_b))
    return u @ pw2_w + pw2_b


def route(t, router_w, router_b):
    n = t.shape[0]
    scores = jax.nn.sigmoid((t @ router_w).astype(jnp.float32))
    sel = (scores + router_b.astype(jnp.float32)).reshape(n, N_GROUPS, EXPERTS_PER_GROUP)
    grp_score = jnp.sum(lax.top_k(sel, TOP_K)[0], axis=-1)
    g_star = jnp.argmax(grp_score, axis=-1)
    in_grp = jnp.take_along_axis(sel, g_star[:, None, None], axis=1)[:, 0]
    _, loc = lax.top_k(in_grp, TOP_K)
    idx = g_star[:, None] * EXPERTS_PER_GROUP + loc
    w = jnp.take_along_axis(scores, idx, axis=-1)
    w = w / jnp.sum(w, axis=-1, keepdims=True)
    return jnp.sum(jax.nn.one_hot(idx, N_EXPERTS, dtype=jnp.float32) * w[..., None], axis=1)


def moe_ffn(h, router_w, router_b, w1, w3, w2):
    shape = h.shape
    t = h.reshape(-1, shape[-1])
    gates = route(t, router_w, router_b).astype(t.dtype)
    out = jnp.zeros_like(t)
    for e in range(N_EXPERTS):
        y = (jax.nn.silu(t @ w1[e]) * (t @ w3[e])) @ w2[e]
        out = out + gates[:, e:e + 1] * y
    return out.reshape(shape)


def setup_inputs(seed: int = 0) -> dict:
    key = jax.random.key(seed)
    ks = iter(jax.random.split(key, 32))
    nrm = lambda shape, s: jax.random.normal(next(ks), shape, jnp.float32) * s
    d = D_MODEL
    return {
        "x": nrm((BATCH, SEQ, d), 1.0),
        "c": nrm((BATCH, d), 1.0),
        "ctx": nrm((BATCH, CTX_LEN, d), 1.0),
        "c_ctx": nrm((d,), 1.0),
        "norm1_g": 1.0 + nrm((DEPTH, d), 0.02),
        "norm2_g": 1.0 + nrm((DEPTH, d), 0.02),
        "w_mod": nrm((DEPTH, d, 6 * d), 0.5 * d ** -0.5),
        "b_mod": nrm((DEPTH, 6 * d), 0.02),
        "na_w_in": nrm((N_EVEN, d, IN_WIDTH), d ** -0.5),
        "na_rpb": nrm((N_EVEN, NA_HEADS, 2 * NA_KR_MAX - 1, 2 * NA_KC - 1), 0.02),
        "sc_conv_w": nrm((N_EVEN, SC_CONV_W, SC_WIDTH), SC_CONV_W ** -0.5),
        "ab_w_out": nrm((N_EVEN, d, d), d ** -0.5),
        "cf_pw1_w": nrm((N_ODD, d, 2 * d), d ** -0.5),
        "cf_pw1_b": nrm((N_ODD, 2 * d), 0.02),
        "cf_dw_w": nrm((N_ODD, CONF_CONV_W, d), CONF_CONV_W ** -0.5),
        "cf_dw_b": nrm((N_ODD, d), 0.02),
        "cf_ln_g": 1.0 + nrm((N_ODD, d), 0.02),
        "cf_ln_b": nrm((N_ODD, d), 0.02),
        "cf_pw2_w": nrm((N_ODD, d, d), d ** -0.5),
        "cf_pw2_b": nrm((N_ODD, d), 0.02),
        "router_w": nrm((d, N_EXPERTS), d ** -0.5),
        "router_b": nrm((N_EXPERTS,), 0.01),
        "moe_w1": nrm((DEPTH, N_EXPERTS, d, EXPERT_FF), d ** -0.5),
        "moe_w3": nrm((DEPTH, N_EXPERTS, d, EXPERT_FF), d ** -0.5),
        "moe_w2": nrm((DEPTH, N_EXPERTS, EXPERT_FF, d), EXPERT_FF ** -0.5),
        "final_g": 1.0 + nrm((d,), 0.02),
    }


def reference(x, c, ctx, c_ctx, norm1_g, norm2_g, w_mod, b_mod, na_w_in, na_rpb, sc_conv_w,
              ab_w_out, cf_pw1_w, cf_pw1_b, cf_dw_w, cf_dw_b, cf_ln_g, cf_ln_b, cf_pw2_w,
              cf_pw2_b, router_w, router_b, moe_w1, moe_w3, moe_w2, final_g):
    silu_c = jax.nn.silu(c)
    silu_cc = jax.nn.silu(c_ctx)
    cs = ctx
    for i in range(DEPTH):
        j = i // 2
        even = i % 2 == 0
        ctx_live = any(m % 2 == 0 for m in range(i + 1, DEPTH))
        mx = (silu_c @ w_mod[i] + b_mod[i])[:, None, :]
        sh1, sc1, g1, sh2, sc2, g2 = jnp.split(mx, 6, axis=-1)
        hx = modulate(rms_norm(x, norm1_g[i]), sh1, sc1)
        if even or ctx_live:
            mc = silu_cc @ w_mod[i] + b_mod[i]
            csh1, csc1, cg1, csh2, csc2, cg2 = jnp.split(mc, 6, axis=-1)
            hc = modulate(rms_norm(cs, norm1_g[i]), csh1, csc1)
        if even:
            y_x, y_c = even_mixer(hx, hc, na_w_in[j], na_rpb[j], sc_conv_w[j], ab_w_out[j], ctx_live)
        else:
            conf = (cf_pw1_w[j], cf_pw1_b[j], cf_dw_w[j], cf_dw_b[j], cf_ln_g[j], cf_ln_b[j],
                    cf_pw2_w[j], cf_pw2_b[j])
            y_x = conformer_conv(hx, *conf)
            y_c = conformer_conv(hc, *conf) if ctx_live else None
        x = x + g1 * y_x
        x = x + g2 * moe_ffn(modulate(rms_norm(x, norm2_g[i]), sh2, sc2),
                             router_w, router_b, moe_w1[i], moe_w3[i], moe_w2[i])
        if ctx_live:
            cs = cs + cg1 * y_c
            cs = cs + cg2 * moe_ffn(modulate(rms_norm(cs, norm2_g[i]), csh2, csc2),
                                    router_w, router_b, moe_w1[i], moe_w3[i], moe_w2[i])
    return rms_norm(x, final_g)
```

```python
import functools

import numpy as np
import jax
import jax.numpy as jnp
from jax import lax
from jax.experimental import pallas as pl
from jax.experimental.pallas import tpu as pltpu

F32 = jnp.float32
BF16 = jnp.bfloat16
HIGHEST = lax.Precision.HIGHEST

GRID_W = 64
NA_HEADS = 8
NA_HEAD_DIM = 64
NA_WIDTH = NA_HEADS * NA_HEAD_DIM
NA_KR = 8
NA_KC = 16
N_EXPERTS = 16
N_GROUPS = 4
EXPERTS_PER_GROUP = N_EXPERTS // N_GROUPS
EPS = 1e-6
NEG = -1e30

LANES = 128
SUBLANES_F32 = 8
SUBLANES_BF16 = 16

TM_PROJ = 512
TM_CONF = 256
ATT_ROWS = 8
ATT_WIN = 2 * ATT_ROWS
ATT_KBLK = 256
TMD = 512
CHUNK = SUBLANES_BF16
CMAX = (2 * TMD + (N_EXPERTS - 1) * CHUNK + CHUNK - 1) // CHUNK + 1
RCOMP = CMAX * CHUNK
TME = 256
CONV_RC = 64
HALO = SUBLANES_BF16
VMEM_LIMIT = 48 * 1024 * 1024


def _cparams(sem):
    return pltpu.CompilerParams(dimension_semantics=sem, vmem_limit_bytes=VMEM_LIMIT)


def _dot(a, b):
    return jnp.dot(a, b, preferred_element_type=F32)


def _dot_nt(a, b, precision=None):
    return lax.dot_general(a, b, (((1,), (1,)), ((), ())), precision=precision,
                           preferred_element_type=F32)


def _sigmoid(x):
    return 1.0 / (1.0 + jnp.exp(-x))


def _norm_mod(x, g, sh, sc):
    ms = jnp.mean(x * x, axis=-1, keepdims=True)
    y = x * lax.rsqrt(ms + EPS) * g
    return y * (1.0 + sc) + sh


def _mod_kernel(c_ref, w_ref, b_ref, o_ref):
    c = c_ref[...]
    s = c * _sigmoid(c)
    o_ref[...] = jnp.dot(s, w_ref[...], precision=HIGHEST, preferred_element_type=F32) + b_ref[...]


def _modulation(c_all, w_mod, b_mod):
    depth, d, n6 = w_mod.shape
    mr = c_all.shape[0]
    tn = 1024
    return pl.pallas_call(
        _mod_kernel,
        grid=(depth, n6 // tn),
        in_specs=[pl.BlockSpec((mr, d), lambda i, j: (0, 0)),
                  pl.BlockSpec((None, d, tn), lambda i, j: (i, 0, j)),
                  pl.BlockSpec((None, 1, tn), lambda i, j: (i, 0, j))],
        out_specs=pl.BlockSpec((None, mr, tn), lambda i, j: (i, 0, j)),
        out_shape=jax.ShapeDtypeStruct((depth, mr, n6), F32),
        compiler_params=_cparams(("arbitrary", "arbitrary")),
        name="modulation",
    )(c_all, w_mod, b_mod.reshape(depth, 1, n6))


def _inproj_kernel(x_ref, g_ref, sh_ref, sc_ref, w_ref, q_ref, k_ref, v_ref, gb_ref, u_ref):
    h = _norm_mod(x_ref[...], g_ref[...], sh_ref[...], sc_ref[...]).astype(BF16)
    w = NA_WIDTH
    q_ref[...] = (_dot(h, w_ref[:, 0:w]) * (NA_HEAD_DIM ** -0.5)).astype(BF16)
    k_ref[...] = _dot(h, w_ref[:, w:2 * w]).astype(BF16)
    v_ref[...] = _dot(h, w_ref[:, 2 * w:3 * w]).astype(BF16)
    gb_ref[...] = _dot(h, w_ref[:, 3 * w:4 * w]).astype(BF16)
    u_ref[...] = (_dot(h, w_ref[:, 4 * w:5 * w]) * _dot(h, w_ref[:, 5 * w:6 * w])).astype(BF16)


def _inproj(x2, g, sh, sc, w_bf, tpb):
    n, d = x2.shape
    tm = TM_PROJ
    vec = pl.BlockSpec((None, 1, d), lambda t: (t // tpb, 0, 0))
    out = jax.ShapeDtypeStruct((n, NA_WIDTH), BF16)
    ospec = pl.BlockSpec((tm, NA_WIDTH), lambda t: (t, 0))
    return pl.pallas_call(
        _inproj_kernel,
        grid=(n // tm,),
        in_specs=[pl.BlockSpec((tm, d), lambda t: (t, 0)),
                  pl.BlockSpec((1, d), lambda t: (0, 0)), vec, vec,
                  pl.BlockSpec(w_bf.shape, lambda t: (0, 0))],
        out_specs=[ospec] * 5,
        out_shape=[out] * 5,
        compiler_params=_cparams(("arbitrary",)),
        name="inproj",
    )(x2, g, sh, sc, w_bf)


def _ctxproj_kernel(x_ref, g_ref, sh_ref, sc_ref, w_ref, k_ref, v_ref):
    h = _norm_mod(x_ref[...], g_ref[...], sh_ref[...], sc_ref[...]).astype(BF16)
    w = NA_WIDTH
    k_ref[...] = _dot(h, w_ref[:, 0:w]).astype(BF16)
    v_ref[...] = _dot(h, w_ref[:, w:2 * w]).astype(BF16)


def _ctxproj(ctx2, g, sh, sc, w_bf, l):
    n, d = ctx2.shape
    one = pl.BlockSpec((1, d), lambda t: (0, 0))
    out = jax.ShapeDtypeStruct((n, NA_WIDTH), BF16)
    ospec = pl.BlockSpec((l, NA_WIDTH), lambda t: (t, 0))
    return pl.pallas_call(
        _ctxproj_kernel,
        grid=(n // l,),
        in_specs=[pl.BlockSpec((l, d), lambda t: (t, 0)), one, one, one,
                  pl.BlockSpec(w_bf.shape, lambda t: (0, 0))],
        out_specs=[ospec] * 2,
        out_shape=[out] * 2,
        compiler_params=_cparams(("arbitrary",)),
        name="ctxproj",
    )(ctx2, g, sh, sc, w_bf)


def _attn_bias_tables(rpb, rows):
    h = rpb.shape[0]
    nb = rows // ATT_ROWS
    cols = np.arange(GRID_W)
    cs = np.clip(cols - NA_KC // 2, 0, GRID_W - NA_KC)
    valid_c = (cols[None, :] >= cs[:, None]) & (cols[None, :] < cs[:, None] + NA_KC)
    dc = np.clip(cols[None, :] - cols[:, None] + (NA_KC - 1), 0, 2 * NA_KC - 2)
    tm = rpb.astype(F32)[:, :, dc]
    tm = jnp.where(valid_c[None, None], tm, NEG)
    n_dr = 2 * NA_KR - 1
    tm = jnp.concatenate([tm, jnp.full((h, 1, GRID_W, GRID_W), NEG, F32)], axis=1)
    dr_idx = np.full((3, ATT_ROWS, ATT_WIN), n_dr, np.int32)
    for ty, j in enumerate((0, 1, nb - 1)):
        ws = int(np.clip(ATT_ROWS * j - NA_KR // 2, 0, rows - ATT_WIN))
        for qr in range(ATT_ROWS):
            r = ATT_ROWS * j + qr
            rs = int(np.clip(r - NA_KR // 2, 0, rows - NA_KR))
            for kr in range(ATT_WIN):
                key_row = ws + kr
                if rs <= key_row < rs + NA_KR:
                    dr_idx[ty, qr, kr] = key_row - r + (NA_KR - 1)
    tab = tm[:, dr_idx]
    tab = jnp.transpose(tab, (0, 1, 2, 4, 3, 5))
    tab = tab.reshape(h // 2, 2, 3, ATT_ROWS * GRID_W, ATT_WIN * GRID_W)
    return jnp.transpose(tab, (0, 2, 1, 3, 4))


def _attn_kernel(q_ref, k0_ref, k1_ref, k2_ref, k3_ref, kc_ref,
                 v0_ref, v1_ref, v2_ref, v3_ref, vc_ref, tab_ref, o_ref):
    q = q_ref[...]
    lane = lax.broadcasted_iota(jnp.int32, (1, LANES), 1)
    kb = [k0_ref[...], k1_ref[...], k2_ref[...], k3_ref[...]]
    vb = [v0_ref[...], v1_ref[...], v2_ref[...], v3_ref[...]]
    kc = kc_ref[...]
    vc = vc_ref[...]
    acc = jnp.zeros(q.shape, F32)
    for hh in range(2):
        hm = jnp.where((lane >= hh * NA_HEAD_DIM) & (lane < (hh + 1) * NA_HEAD_DIM), 1.0, 0.0).astype(BF16)
        qh = q * hm
        s = [_dot_nt(qh, kb[i]) + tab_ref[hh, :, i * ATT_KBLK:(i + 1) * ATT_KBLK] for i in range(4)]
        s.append(_dot_nt(qh, kc))
        m = jnp.max(s[0], axis=-1, keepdims=True)
        for si in s[1:]:
            m = jnp.maximum(m, jnp.max(si, axis=-1, keepdims=True))
        p = [jnp.exp(si - m) for si in s]
        l = jnp.sum(p[0], axis=-1, keepdims=True)
        for pi in p[1:]:
            l = l + jnp.sum(pi, axis=-1, keepdims=True)
        o = _dot(p[4].astype(BF16), vc * hm)
        for i in range(4):
            o = o + _dot(p[i].astype(BF16), vb[i] * hm)
        acc = acc + o / l
    o_ref[...] = acc.astype(o_ref.dtype)


def _attention(q, k, v, kc, vc, tab, b, s, l):
    n = q.shape[0]
    rows = s // GRID_W
    nb = rows // ATT_ROWS
    qblk = ATT_ROWS * GRID_W
    kpb = s // ATT_KBLK
    hp = NA_HEADS // 2

    def kmap(i):
        def f(h, j, bb):
            start = jnp.clip(2 * j - 1, 0, kpb - 4)
            return (bb * kpb + start + i, h)
        return f

    qspec = pl.BlockSpec((qblk, LANES), lambda h, j, bb: (bb * nb + j, h))
    kspecs = [pl.BlockSpec((ATT_KBLK, LANES), kmap(i)) for i in range(4)]
    cspec = pl.BlockSpec((l, LANES), lambda h, j, bb: (bb, h))
    tspec = pl.BlockSpec((None, None, 2, qblk, ATT_WIN * GRID_W),
                         lambda h, j, bb: (h, jnp.where(j == 0, 0, jnp.where(j == nb - 1, 2, 1)), 0, 0, 0))
    return pl.pallas_call(
        _attn_kernel,
        grid=(hp, nb, b),
        in_specs=[qspec] + kspecs + [cspec] + kspecs + [cspec, tspec],
        out_specs=qspec,
        out_shape=jax.ShapeDtypeStruct((n, NA_WIDTH), BF16),
        compiler_params=_cparams(("arbitrary", "arbitrary", "arbitrary")),
        name="nbr_attention",
    )(q, k, k, k, k, kc, v, v, v, v, vc, tab)


def _residual_norm_route(x, y, g1, n2g, sh2, sc2, rwt, rb, xo_ref, ho_ref, gate_ref, sel_ref):
    x1 = x + g1 * y
    xo_ref[...] = x1
    h2 = _norm_mod(x1, n2g, sh2, sc2)
    ho_ref[...] = h2.astype(BF16)
    logits = _dot_nt(rwt, h2, precision=HIGHEST)
    scores = _sigmoid(logits)
    sel = scores + rb
    r = [sel[e:e + 1, :] for e in range(N_EXPERTS)]
    sc = [scores[e:e + 1, :] for e in range(N_EXPERTS)]
    grp = []
    for g in range(N_GROUPS):
        a, b_, c, d = r[4 * g:4 * g + 4]
        hi1, lo1 = jnp.maximum(a, b_), jnp.minimum(a, b_)
        hi2, lo2 = jnp.maximum(c, d), jnp.minimum(c, d)
        m1 = jnp.maximum(hi1, hi2)
        m2 = jnp.maximum(jnp.minimum(hi1, hi2), jnp.maximum(lo1, lo2))
        grp.append(m1 + m2)
    best = grp[0]
    gi = jnp.zeros(best.shape, jnp.int32)
    for g in range(1, N_GROUPS):
        upd = grp[g] > best
        best = jnp.where(upd, grp[g], best)
        gi = jnp.where(upd, g, gi)
    chosen = []
    for e in range(N_EXPERTS):
        g = e // EXPERTS_PER_GROUP
        rank = jnp.zeros(best.shape, jnp.int32)
        for e2 in range(EXPERTS_PER_GROUP * g, EXPERTS_PER_GROUP * (g + 1)):
            if e2 == e:
                continue
            ahead = (r[e2] > r[e]) | ((r[e2] == r[e]) & (e2 < e))
            rank = rank + jnp.where(ahead, 1, 0)
        chosen.append((gi == g) & (rank < 2))
    wsel = [jnp.where(chosen[e], sc[e], 0.0) for e in range(N_EXPERTS)]
    denom = wsel[0]
    for e in range(1, N_EXPERTS):
        denom = denom + wsel[e]
    for e in range(N_EXPERTS):
        gate_ref[e:e + 1, :] = wsel[e] / denom
        sel_ref[e:e + 1, :] = jnp.where(chosen[e], 1, 0).astype(jnp.int32)


def _tail_specs(n, d, tm, tpb):
    vec = pl.BlockSpec((None, 1, d), lambda t: (t // tpb, 0, 0))
    one = pl.BlockSpec((1, d), lambda t: (0, 0))
    in_specs = [pl.BlockSpec((tm, d), lambda t: (t, 0)),
                vec, one, vec, vec,
                pl.BlockSpec((N_EXPERTS, d), lambda t: (0, 0)),
                pl.BlockSpec((N_EXPERTS, 1), lambda t: (0, 0))]
    out_specs = [pl.BlockSpec((tm, d), lambda t: (t, 0)),
                 pl.BlockSpec((tm, d), lambda t: (t, 0)),
                 pl.BlockSpec((N_EXPERTS, tm), lambda t: (0, t)),
                 pl.BlockSpec((N_EXPERTS, tm), lambda t: (0, t))]
    out_shape = [jax.ShapeDtypeStruct((n, d), F32), jax.ShapeDtypeStruct((n, d), BF16),
                 jax.ShapeDtypeStruct((N_EXPERTS, n), F32), jax.ShapeDtypeStruct((N_EXPERTS, n), jnp.int32)]
    return in_specs, out_specs, out_shape


def _outproj_kernel(tpb, a_ref, gb_ref, u_ref, up_ref, un_ref, cw_ref, wa_ref, wb_ref,
                    x_ref, g1_ref, n2g_ref, sh2_ref, sc2_ref, rwt_ref, rb_ref,
                    xo_ref, ho_ref, gate_ref, sel_ref):
    t = pl.program_id(0)
    tm = u_ref.shape[0]
    u = u_ref[...].astype(F32)
    keep_prev = jnp.where(t % tpb == 0, 0.0, 1.0)
    keep_next = jnp.where(t % tpb == tpb - 1, 0.0, 1.0)
    prev_row = up_ref[HALO - 1:HALO, :].astype(F32) * keep_prev
    next_row = un_ref[0:1, :].astype(F32) * keep_next
    row = lax.broadcasted_iota(jnp.int32, u.shape, 0)
    u_m1 = jnp.where(row == 0, prev_row, pltpu.roll(u, 1, 0))
    u_p1 = jnp.where(row == tm - 1, next_row, pltpu.roll(u, tm - 1, 0))
    conv = cw_ref[0:1, :] * u_m1 + cw_ref[1:2, :] * u + cw_ref[2:3, :] * u_p1
    bx = (gb_ref[...].astype(F32) * conv).astype(BF16)
    y = _dot(a_ref[...], wa_ref[...]) + _dot(bx, wb_ref[...])
    _residual_norm_route(x_ref[...], y, g1_ref[...], n2g_ref[...], sh2_ref[...], sc2_ref[...],
                         rwt_ref[...], rb_ref[...], xo_ref, ho_ref, gate_ref, sel_ref)


def _outproj(a, gb, u, cw, wa, wb, x2, g1, n2g, sh2, sc2, rwt, rb, tpb):
    n, d = x2.shape
    tm = TM_PROJ
    w = NA_WIDTH
    hb = tm // HALO
    nh = n // HALO
    half = pl.BlockSpec((tm, w), lambda t: (t, 0))
    tail_in, out_specs, out_shape = _tail_specs(n, d, tm, tpb)
    in_specs = [half, half, half,
                pl.BlockSpec((HALO, w), lambda t: (jnp.maximum(t * hb - 1, 0), 0)),
                pl.BlockSpec((HALO, w), lambda t: (jnp.minimum((t + 1) * hb, nh - 1), 0)),
                pl.BlockSpec(cw.shape, lambda t: (0, 0)),
                pl.BlockSpec(wa.shape, lambda t: (0, 0)),
                pl.BlockSpec(wb.shape, lambda t: (0, 0))] + tail_in
    return pl.pallas_call(
        functools.partial(_outproj_kernel, tpb),
        grid=(n // tm,),
        in_specs=in_specs, out_specs=out_specs, out_shape=out_shape,
        compiler_params=_cparams(("arbitrary",)),
        name="outproj_route",
    )(a, gb, u, u, u, cw, wa, wb, x2, g1, n2g, sh2, sc2, rwt, rb)


def _pw1_kernel(x_ref, g_ref, sh_ref, sc_ref, w_ref, b_ref, u_ref):
    d = x_ref.shape[1]
    h = _norm_mod(x_ref[...], g_ref[...], sh_ref[...], sc_ref[...]).astype(BF16)
    a = _dot(h, w_ref[:, 0:d]) + b_ref[:, 0:d]
    g = _dot(h, w_ref[:, d:2 * d]) + b_ref[:, d:2 * d]
    u_ref[...] = (a * _sigmoid(g)).astype(BF16)


def _pw1(x2, g, sh, sc, w_bf, b, tpb):
    n, d = x2.shape
    tm = TM_PROJ
    vec = pl.BlockSpec((None, 1, d), lambda t: (t // tpb, 0, 0))
    return pl.pallas_call(
        _pw1_kernel,
        grid=(n // tm,),
        in_specs=[pl.BlockSpec((tm, d), lambda t: (t, 0)),
                  pl.BlockSpec((1, d), lambda t: (0, 0)), vec, vec,
                  pl.BlockSpec(w_bf.shape, lambda t: (0, 0)),
                  pl.BlockSpec(b.shape, lambda t: (0, 0))],
        out_specs=pl.BlockSpec((tm, d), lambda t: (t, 0)),
        out_shape=jax.ShapeDtypeStruct((n, d), BF16),
        compiler_params=_cparams(("arbitrary",)),
        name="conf_pw1_glu",
    )(x2, g, sh, sc, w_bf, b)


def _conf_kernel(tpb, u_ref, up_ref, un_ref, dww_ref, dwb_ref, lng_ref, lnb_ref, w2_ref, b2_ref,
                 x_ref, g1_ref, n2g_ref, sh2_ref, sc2_ref, rwt_ref, rb_ref,
                 xo_ref, ho_ref, gate_ref, sel_ref, ue_ref, sh_ref, conv_ref):
    t = pl.program_id(0)
    tm, d = u_ref.shape
    taps = dww_ref.shape[0]
    keep_prev = jnp.where(t % tpb == 0, 0.0, 1.0)
    keep_next = jnp.where(t % tpb == tpb - 1, 0.0, 1.0)
    ue_ref[0:HALO, :] = up_ref[...].astype(F32) * keep_prev
    ue_ref[HALO:HALO + tm, :] = u_ref[...].astype(F32)
    ue_ref[HALO + tm:HALO + tm + HALO, :] = un_ref[...].astype(F32) * keep_next
    ext = tm + 2 * HALO - SUBLANES_F32
    for s in range(SUBLANES_F32):
        sh_ref[s, :, :] = ue_ref[s:s + ext, :]
    off0 = HALO - (taps - 1) // 2
    for cb in range(d // LANES):
        cols = slice(cb * LANES, (cb + 1) * LANES)
        wv = dww_ref[:, cols]

        def body(rc, carry, cols=cols, wv=wv):
            r0 = pl.multiple_of(rc * CONV_RC, CONV_RC)
            acc = jnp.zeros((CONV_RC, LANES), F32)
            for k in range(taps):
                s, a = (k + off0) % SUBLANES_F32, (k + off0) // SUBLANES_F32
                acc = acc + wv[k:k + 1, :] * sh_ref[s, pl.ds(r0 + SUBLANES_F32 * a, CONV_RC), cols]
            conv_ref[pl.ds(r0, CONV_RC), cols] = acc
            return carry

        lax.fori_loop(0, tm // CONV_RC, body, 0)
    c = conv_ref[...] + dwb_ref[...]
    mu = jnp.mean(c, axis=-1, keepdims=True)
    cc = c - mu
    var = jnp.mean(cc * cc, axis=-1, keepdims=True)
    z = cc * lax.rsqrt(var + EPS) * lng_ref[...] + lnb_ref[...]
    z = (z * _sigmoid(z)).astype(BF16)
    y = _dot(z, w2_ref[...]) + b2_ref[...]
    _residual_norm_route(x_ref[...], y, g1_ref[...], n2g_ref[...], sh2_ref[...], sc2_ref[...],
                         rwt_ref[...], rb_ref[...], xo_ref, ho_ref, gate_ref, sel_ref)


def _conf(u, dww, dwb, lng, lnb, w2, b2, x2, g1, n2g, sh2, sc2, rwt, rb, tpb):
    n, d = x2.shape
    tm = TM_CONF
    hb = tm // HALO
    nh = n // HALO
    one = pl.BlockSpec((1, d), lambda t: (0, 0))
    tail_in, out_specs, out_shape = _tail_specs(n, d, tm, tpb)
    in_specs = [pl.BlockSpec((tm, d), lambda t: (t, 0)),
                pl.BlockSpec((HALO, d), lambda t: (jnp.maximum(t * hb - 1, 0), 0)),
                pl.BlockSpec((HALO, d), lambda t: (jnp.minimum((t + 1) * hb, nh - 1), 0)),
                pl.BlockSpec(dww.shape, lambda t: (0, 0)), one, one, one,
                pl.BlockSpec(w2.shape, lambda t: (0, 0)), one] + tail_in
    ext = tm + 2 * HALO - SUBLANES_F32
    return pl.pallas_call(
        functools.partial(_conf_kernel, tpb),
        grid=(n // tm,),
        in_specs=in_specs, out_specs=out_specs, out_shape=out_shape,
        scratch_shapes=[pltpu.VMEM((tm + 2 * HALO, d), F32),
                        pltpu.VMEM((SUBLANES_F32, ext, d), F32),
                        pltpu.VMEM((tm, d), F32)],
        compiler_params=_cparams(("arbitrary",)),
        name="conf_conv_route",
    )(u, u, u, dww, dwb, lng, lnb, w2, b2, x2, g1, n2g, sh2, sc2, rwt, rb)


def _moe_plan(sel, gates, n):
    e_n = N_EXPERTS
    t_n = n // TMD
    s = sel.reshape(e_n, t_n, TMD)
    gt = gates.reshape(e_n, t_n, TMD)
    cnt = jnp.sum(s, axis=-1)
    seg = (cnt + CHUNK - 1) // CHUNK * CHUNK
    rank = jnp.cumsum(s, axis=-1) - s
    lo = jnp.cumsum(seg, axis=0) - seg
    lp = lo[:, :, None] + rank
    slot = jnp.cumsum(s, axis=0) - s
    is0 = (s > 0) & (slot == 0)
    is1 = (s > 0) & (slot == 1)

    def pick(mask, val, default):
        return jnp.where(jnp.any(mask, axis=0), jnp.sum(jnp.where(mask, val, 0), axis=0), default)

    lp0 = pick(is0, lp, -1).astype(jnp.int32)
    lp1 = pick(is1, lp, -1).astype(jnp.int32)
    g0 = pick(is0, gt, 0.0)
    g1 = pick(is1, gt, 0.0)
    etot = jnp.sum(seg, axis=1)
    epad = (etot + TME - 1) // TME * TME
    eend = jnp.cumsum(epad)
    estart = eend - epad
    go = estart[:, None] + jnp.cumsum(seg, axis=1) - seg
    nchunk = (jnp.sum(seg, axis=0) // CHUNK).astype(jnp.int32)
    c_rows = jnp.arange(CMAX, dtype=jnp.int32) * CHUNK
    lend = (lo + seg).T
    owner = jnp.sum((lend[:, None, :] <= c_rows[None, :, None]).astype(jnp.int32), axis=-1)
    owner = jnp.minimum(owner, e_n - 1)
    gsel = jnp.take_along_axis(go.T, owner, axis=1)
    lsel = jnp.take_along_axis(lo.T, owner, axis=1)
    rt = _sorted_rows(n)
    cmap = jnp.clip((gsel + c_rows[None, :] - lsel) // CHUNK, 0, rt // CHUNK - 1).astype(jnp.int32)
    npad = (epad - etot) // CHUNK
    pend = jnp.cumsum(npad)
    pmax = e_n * (TME // CHUNK - 1)
    kk = jnp.arange(pmax, dtype=jnp.int32)
    pown = jnp.minimum(jnp.sum((pend[None, :] <= kk[:, None]).astype(jnp.int32), axis=-1), e_n - 1)
    padmap = ((estart + etot)[pown] // CHUNK + kk - (pend - npad)[pown])
    padmap = jnp.clip(padmap, 0, rt // CHUNK - 1).astype(jnp.int32)
    npadtot = pend[-1:].astype(jnp.int32)
    nt = rt // TME
    tile_row = jnp.arange(nt, dtype=jnp.int32) * TME
    texp = jnp.minimum(jnp.sum((eend[None, :] <= tile_row[:, None]).astype(jnp.int32), axis=-1), e_n - 1)
    nact = (eend[-1:] // TME).astype(jnp.int32)
    return dict(lp0=lp0, lp1=lp1, g0=g0, g1=g1, nchunk=nchunk, cmap=cmap.reshape(-1),
                padmap=padmap, npadtot=npadtot, texp=texp.astype(jnp.int32), nact=nact)


def _sorted_rows(n):
    t_n = n // TMD
    rows = 2 * n + (CHUNK - 1) * t_n * N_EXPERTS + N_EXPERTS * (TME - CHUNK)
    return (rows + TME - 1) // TME * TME


def _chunk_copy(src_ref, dst_ref, sem):
    return pltpu.make_async_copy(src_ref, dst_ref, sem)


def _dispatch_kernel(nchunk_ref, cmap_ref, npad_ref, padmap_ref, nact_ref,
                     h_ref, lp0_ref, lp1_ref, xs_ref, xc_ref, z_ref, sem, zsem):
    t = pl.program_id(0)
    nt = pl.num_programs(0)
    slot = t % 2
    ntail = xs_ref.shape[0] // TME - nact_ref[0]

    def pad_copy(k):
        g = pl.multiple_of(padmap_ref[k] * CHUNK, CHUNK)
        return _chunk_copy(z_ref.at[0:CHUNK, :], xs_ref.at[pl.ds(g, CHUNK), :], zsem.at[0])

    def tail_copy(k):
        g = pl.multiple_of((nact_ref[0] + k) * TME, TME)
        return _chunk_copy(z_ref, xs_ref.at[pl.ds(g, TME), :], zsem.at[0])

    @pl.when(t == 0)
    def _():
        z_ref[...] = jnp.zeros(z_ref.shape, z_ref.dtype)
        lax.fori_loop(0, npad_ref[0], lambda k, c: (pad_copy(k).start(), c)[1], 0)
        lax.fori_loop(0, ntail, lambda k, c: (tail_copy(k).start(), c)[1], 0)

    rows = lax.broadcasted_iota(jnp.int32, (RCOMP, TMD), 0)
    onehot = jnp.where((lp0_ref[...] == rows) | (lp1_ref[...] == rows), 1.0, 0.0).astype(BF16)
    xc_ref[slot] = _dot(onehot, h_ref[...]).astype(BF16)

    def out_copy(tt, sl, c):
        g = pl.multiple_of(cmap_ref[tt * CMAX + c] * CHUNK, CHUNK)
        l = pl.multiple_of(c * CHUNK, CHUNK)
        return _chunk_copy(xc_ref.at[sl, pl.ds(l, CHUNK), :], xs_ref.at[pl.ds(g, CHUNK), :], sem.at[sl])

    lax.fori_loop(0, nchunk_ref[t], lambda c, cr: (out_copy(t, slot, c).start(), cr)[1], 0)

    @pl.when(t > 0)
    def _():
        lax.fori_loop(0, nchunk_ref[t - 1], lambda c, cr: (out_copy(t - 1, 1 - slot, c).wait(), cr)[1], 0)

    @pl.when(t == 0)
    def _():
        lax.fori_loop(0, npad_ref[0], lambda k, c: (pad_copy(k).wait(), c)[1], 0)
        lax.fori_loop(0, ntail, lambda k, c: (tail_copy(k).wait(), c)[1], 0)

    @pl.when(t == nt - 1)
    def _():
        lax.fori_loop(0, nchunk_ref[t], lambda c, cr: (out_copy(t, slot, c).wait(), cr)[1], 0)


def _dispatch(h2, plan):
    n, d = h2.shape
    t_n = n // TMD
    rt = _sorted_rows(n)
    grid_spec = pltpu.PrefetchScalarGridSpec(
        num_scalar_prefetch=5,
        grid=(t_n,),
        in_specs=[pl.BlockSpec((TMD, d), lambda t, *_: (t, 0)),
                  pl.BlockSpec((None, 1, TMD), lambda t, *_: (t, 0, 0)),
                  pl.BlockSpec((None, 1, TMD), lambda t, *_: (t, 0, 0))],
        out_specs=pl.BlockSpec(memory_space=pl.ANY),
        scratch_shapes=[pltpu.VMEM((2, RCOMP, d), BF16),
                        pltpu.VMEM((TME, d), BF16),
                        pltpu.SemaphoreType.DMA((2,)),
                        pltpu.SemaphoreType.DMA((1,))],
    )
    return pl.pallas_call(
        _dispatch_kernel,
        grid_spec=grid_spec,
        out_shape=jax.ShapeDtypeStruct((rt, d), BF16),
        compiler_params=_cparams(("arbitrary",)),
        name="moe_dispatch",
    )(plan["nchunk"], plan["cmap"], plan["npadtot"], plan["padmap"], plan["nact"],
      h2, plan["lp0"].reshape(t_n, 1, TMD), plan["lp1"].reshape(t_n, 1, TMD))


def _expert_kernel(texp_ref, nact_ref, x_ref, w1_ref, w3_ref, w2_ref, y_ref, w1b, w3b, w2b):
    i = pl.program_id(0)

    @pl.when(i < nact_ref[0])
    def _():
        prev = texp_ref[jnp.maximum(i - 1, 0)]

        @pl.when((i == 0) | (texp_ref[i] != prev))
        def _():
            w1b[...] = w1_ref[...].astype(BF16)
            w3b[...] = w3_ref[...].astype(BF16)
            w2b[...] = w2_ref[...].astype(BF16)

        x = x_ref[...]
        h = _dot(x, w1b[...])
        g = _dot(x, w3b[...])
        a = (h * _sigmoid(h) * g).astype(BF16)
        y_ref[...] = _dot(a, w2b[...]).astype(y_ref.dtype)

    @pl.when(i >= nact_ref[0])
    def _():
        y_ref[...] = jnp.zeros(y_ref.shape, y_ref.dtype)


def _experts(xs, w1, w3, w2, layer, plan):
    rt, d = xs.shape
    f = w1.shape[-1]
    nt = rt // TME

    def row_map(i, texp, nact):
        return (jnp.minimum(i, nact[0] - 1), 0)

    def w_map(i, texp, nact):
        return (layer, texp[jnp.minimum(i, nact[0] - 1)], 0, 0)

    grid_spec = pltpu.PrefetchScalarGridSpec(
        num_scalar_prefetch=2,
        grid=(nt,),
        in_specs=[pl.BlockSpec((TME, d), row_map),
                  pl.BlockSpec((None, None, d, f), w_map),
                  pl.BlockSpec((None, None, d, f), w_map),
                  pl.BlockSpec((None, None, f, d), w_map)],
        out_specs=pl.BlockSpec((TME, d), lambda i, texp, nact: (i, 0)),
        scratch_shapes=[pltpu.VMEM((d, f), BF16), pltpu.VMEM((d, f), BF16), pltpu.VMEM((f, d), BF16)],
    )
    return pl.pallas_call(
        _expert_kernel,
        grid_spec=grid_spec,
        out_shape=jax.ShapeDtypeStruct((rt, d), BF16),
        compiler_params=_cparams(("arbitrary",)),
        name="moe_experts",
    )(plan["texp"], plan["nact"], xs, w1, w3, w2)


def _combine_kernel(final, nchunk_ref, cmap_ref, ys_ref, lp0_ref, lp1_ref, g0_ref, g1_ref,
                    x_ref, g2_ref, fg_ref, o_ref, yc_ref, sem):
    t = pl.program_id(0)
    nt = pl.num_programs(0)
    slot = t % 2

    def in_copy(tt, sl, c):
        g = pl.multiple_of(cmap_ref[tt * CMAX + c] * CHUNK, CHUNK)
        l = pl.multiple_of(c * CHUNK, CHUNK)
        return _chunk_copy(ys_ref.at[pl.ds(g, CHUNK), :], yc_ref.at[sl, pl.ds(l, CHUNK), :], sem.at[sl])

    @pl.when(t == 0)
    def _():
        yc_ref[...] = jnp.zeros(yc_ref.shape, yc_ref.dtype)
        lax.fori_loop(0, nchunk_ref[0], lambda c, cr: (in_copy(0, 0, c).start(), cr)[1], 0)

    @pl.when(t + 1 < nt)
    def _():
        lax.fori_loop(0, nchunk_ref[t + 1], lambda c, cr: (in_copy(t + 1, 1 - slot, c).start(), cr)[1], 0)

    lax.fori_loop(0, nchunk_ref[t], lambda c, cr: (in_copy(t, slot, c).wait(), cr)[1], 0)

    cols = lax.broadcasted_iota(jnp.int32, (TMD, RCOMP), 1)
    w = jnp.where(lp0_ref[...] == cols, g0_ref[...], 0.0) + jnp.where(lp1_ref[...] == cols, g1_ref[...], 0.0)
    moe = _dot(w.astype(BF16), yc_ref[slot])
    x1 = x_ref[...] + g2_ref[...] * moe
    if final:
        ms = jnp.mean(x1 * x1, axis=-1, keepdims=True)
        x1 = x1 * lax.rsqrt(ms + EPS) * fg_ref[...]
    o_ref[...] = x1


def _combine(ys, plan, x2, g2, fg, tpb_d, final):
    n, d = x2.shape
    t_n = n // TMD
    col = pl.BlockSpec((TMD, 1), lambda t, *_: (t, 0))
    grid_spec = pltpu.PrefetchScalarGridSpec(
        num_scalar_prefetch=2,
        grid=(t_n,),
        in_specs=[pl.BlockSpec(memory_space=pl.ANY), col, col, col, col,
                  pl.BlockSpec((TMD, d), lambda t, *_: (t, 0)),
                  pl.BlockSpec((None, 1, d), lambda t, *_: (t // tpb_d, 0, 0)),
                  pl.BlockSpec((1, d), lambda t, *_: (0, 0))],
        out_specs=pl.BlockSpec((TMD, d), lambda t, *_: (t, 0)),
        scratch_shapes=[pltpu.VMEM((2, RCOMP, d), BF16), pltpu.SemaphoreType.DMA((2,))],
    )
    return pl.pallas_call(
        functools.partial(_combine_kernel, final),
        grid_spec=grid_spec,
        out_shape=jax.ShapeDtypeStruct((n, d), F32),
        compiler_params=_cparams(("arbitrary",)),
        name="moe_combine_final" if final else "moe_combine",
    )(plan["nchunk"], plan["cmap"], ys,
      plan["lp0"].reshape(n, 1), plan["lp1"].reshape(n, 1),
      plan["g0"].reshape(n, 1), plan["g1"].reshape(n, 1), x2, g2, fg)


def _moe(x1, h2, gates, sel, w1, w3, w2, layer, g2, fg, tpb_d, final):
    n = x1.shape[0]
    plan = _moe_plan(sel, gates, n)
    xs = _dispatch(h2, plan)
    ys = _experts(xs, w1, w3, w2, layer, plan)
    return _combine(ys, plan, x1, g2, fg, tpb_d, final)


def kernel(x, c, ctx, c_ctx, norm1_g, norm2_g, w_mod, b_mod, na_w_in, na_rpb, sc_conv_w, ab_w_out,
           cf_pw1_w, cf_pw1_b, cf_dw_w, cf_dw_b, cf_ln_g, cf_ln_b, cf_pw2_w, cf_pw2_b,
           router_w, router_b, moe_w1, moe_w3, moe_w2, final_g):
    b, s, d = x.shape
    l = ctx.shape[1]
    n = b * s
    depth = w_mod.shape[0]
    assert depth == 2, "layer pattern (attention layer, Conformer layer) is written out for depth 2"
    assert s % TM_PROJ == 0 and s % TMD == 0 and s % TM_CONF == 0
    rows = s // GRID_W
    assert rows % ATT_ROWS == 0 and rows >= ATT_WIN and s % ATT_KBLK == 0
    tpb = s // TM_PROJ

    mr = (b + 1 + SUBLANES_F32 - 1) // SUBLANES_F32 * SUBLANES_F32
    c_all = jnp.zeros((mr, d), F32).at[:b].set(c).at[b].set(c_ctx)
    mod = _modulation(c_all, w_mod, b_mod)

    def mvec(i, j):
        return mod[i, :b, j * d:(j + 1) * d].reshape(b, 1, d)

    x2 = x.reshape(n, d)
    rwt = router_w.T
    rb = router_b.reshape(N_EXPERTS, 1)
    fg = final_g.reshape(1, d)

    w_in = na_w_in[0].astype(BF16)
    q, k, v, gb, u = _inproj(x2, norm1_g[0].reshape(1, d), mvec(0, 0), mvec(0, 1), w_in, tpb)
    csh = mod[0, b:b + 1, 0:d]
    csc = mod[0, b:b + 1, d:2 * d]
    kc, vc = _ctxproj(ctx.reshape(b * l, d), norm1_g[0].reshape(1, d), csh, csc,
                      w_in[:, NA_WIDTH:3 * NA_WIDTH], l)
    tab = _attn_bias_tables(na_rpb[0], rows)
    a = _attention(q, k, v, kc, vc, tab, b, s, l)
    w_out = ab_w_out[0].astype(BF16)
    x1, h2, gates, sel = _outproj(a, gb, u, sc_conv_w[0], w_out[:NA_WIDTH], w_out[NA_WIDTH:], x2,
                                  mvec(0, 2), norm2_g[0].reshape(1, d), mvec(0, 3), mvec(0, 4),
                                  rwt, rb, tpb)
    x2 = _moe(x1, h2, gates, sel, moe_w1, moe_w3, moe_w2, 0, mvec(0, 5), fg, s // TMD, False)

    u = _pw1(x2, norm1_g[1].reshape(1, d), mvec(1, 0), mvec(1, 1), cf_pw1_w[0].astype(BF16),
             cf_pw1_b[0].reshape(1, 2 * d), tpb)
    x1, h2, gates, sel = _conf(u, cf_dw_w[0], cf_dw_b[0].reshape(1, d), cf_ln_g[0].reshape(1, d),
                               cf_ln_b[0].reshape(1, d), cf_pw2_w[0].astype(BF16),
                               cf_pw2_b[0].reshape(1, d), x2,
                               mvec(1, 2), norm2_g[1].reshape(1, d), mvec(1, 3), mvec(1, 4),
                               rwt, rb, s // TM_CONF)
    out = _moe(x1, h2, gates, sel, moe_w1, moe_w3, moe_w2, 1, mvec(1, 5), fg, s // TMD, True)
    return out.reshape(b, s, d)
```

```python
import functools

import numpy as np
import jax
import jax.numpy as jnp
from jax import lax
from jax.experimental import pallas as pl
from jax.experimental.pallas import tpu as pltpu

F32 = jnp.float32
BF16 = jnp.bfloat16
I32 = jnp.int32

GRID_W = 64
NA_HEADS = 8
NA_HEAD_DIM = 64
NA_WIDTH = NA_HEADS * NA_HEAD_DIM
NA_KR = 8
NA_KC = 16
N_EXPERTS = 16
N_GROUPS = 4
EXPERTS_PER_GROUP = N_EXPERTS // N_GROUPS
N_MOD = 6
EPS = 1e-6
NEG = -1e30
LOG2E = 1.4426950408889634

LANES = 128
SUBLANES_F32 = 8
SUBLANES_BF16 = 16

TM_PROJ = 512
ATT_ROWS = 8
ATT_WIN = 2 * ATT_ROWS
ATT_KBLK = 256
TMD = 512
CHUNK = SUBLANES_BF16
CMAX = (2 * TMD + (N_EXPERTS - 1) * CHUNK + CHUNK - 1) // CHUNK + 1
RCOMP = CMAX * CHUNK
TME = 512
CONV_RC = 128
HALO = SUBLANES_BF16
VMEM_LIMIT = 52 * 1024 * 1024


def _cparams(sem):
    return pltpu.CompilerParams(dimension_semantics=sem, vmem_limit_bytes=VMEM_LIMIT)


def _dot(a, b):
    return jnp.dot(a, b, preferred_element_type=F32)


def _dot_nt(a, b, precision=None):
    return lax.dot_general(a, b, (((1,), (1,)), ((), ())), precision=precision,
                           preferred_element_type=F32)


def _sigmoid(x):
    return 1.0 / (1.0 + jnp.exp(-x))


def _norm_mod(x, g, sh, sc):
    ms = jnp.mean(x * x, axis=-1, keepdims=True)
    y = x * lax.rsqrt(ms + EPS) * g
    return y * (1.0 + sc) + sh


def _split_bf16(a):
    hi = a.astype(BF16)
    lo = (a - hi.astype(F32)).astype(BF16)
    return hi, lo


def _mod_kernel(c_ref, w_ref, b_ref, o_ref):
    c = c_ref[...]
    mr = c.shape[0]
    hi, lo = _split_bf16(c * _sigmoid(c))
    w_hi, w_lo = _split_bf16(w_ref[...])
    r = _dot(jnp.concatenate([hi, lo], axis=0), w_hi)
    o_ref[...] = r[:mr] + r[mr:] + _dot(hi, w_lo) + b_ref[...]


def _modulation(c_all, w_mod, b_mod):
    depth, d, n6 = w_mod.shape
    mr = c_all.shape[0]
    tn = 1024
    return pl.pallas_call(
        _mod_kernel,
        grid=(depth, n6 // tn),
        in_specs=[pl.BlockSpec((mr, d), lambda i, j: (0, 0)),
                  pl.BlockSpec((None, d, tn), lambda i, j: (i, 0, j)),
                  pl.BlockSpec((None, 1, tn), lambda i, j: (i, 0, j))],
        out_specs=pl.BlockSpec((None, mr, tn), lambda i, j: (i, 0, j)),
        out_shape=jax.ShapeDtypeStruct((depth, mr, n6), F32),
        compiler_params=_cparams(("arbitrary", "arbitrary")),
        name="modulation",
    )(c_all, w_mod, b_mod.reshape(depth, 1, n6))


class _Mod:
    def __init__(self, mod, d):
        self.depth, self.mr, _ = mod.shape
        self.d = d
        self.arr = mod.reshape(self.depth * self.mr * N_MOD, 1, d)

    def vec(self, layer, j, tpb):
        mr = self.mr
        return pl.BlockSpec((None, 1, self.d),
                            lambda t, *_: ((layer * mr + t // tpb) * N_MOD + j, 0, 0))

    def row(self, layer, j, r):
        mr = self.mr
        return pl.BlockSpec((None, 1, self.d), lambda t, *_: ((layer * mr + r) * N_MOD + j, 0, 0))


def _inproj_kernel(x_ref, g_ref, sh_ref, sc_ref, w_ref, wkt_ref, q_ref, kt_ref, v_ref, gb_ref, u_ref):
    h = _norm_mod(x_ref[...], g_ref[...], sh_ref[...], sc_ref[...]).astype(BF16)
    w = NA_WIDTH
    q_ref[...] = (_dot(h, w_ref[:, 0:w]) * (NA_HEAD_DIM ** -0.5 * LOG2E)).astype(BF16)
    kt_ref[...] = _dot_nt(wkt_ref[...], h).astype(BF16)
    v_ref[...] = _dot(h, w_ref[:, 2 * w:3 * w]).astype(BF16)
    gb_ref[...] = _dot(h, w_ref[:, 3 * w:4 * w]).astype(BF16)
    u_ref[...] = (_dot(h, w_ref[:, 4 * w:5 * w]) * _dot(h, w_ref[:, 5 * w:6 * w])).astype(BF16)


def _inproj(x2, g, mod, w_bf, wkt_bf, tpb):
    n, d = x2.shape
    tm = TM_PROJ
    out = jax.ShapeDtypeStruct((n, NA_WIDTH), BF16)
    ospec = pl.BlockSpec((tm, NA_WIDTH), lambda t: (t, 0))
    out_t = jax.ShapeDtypeStruct((NA_WIDTH, n), BF16)
    ospec_t = pl.BlockSpec((NA_WIDTH, tm), lambda t: (0, t))
    return pl.pallas_call(
        _inproj_kernel,
        grid=(n // tm,),
        in_specs=[pl.BlockSpec((tm, d), lambda t: (t, 0)),
                  pl.BlockSpec((1, d), lambda t: (0, 0)), mod.vec(0, 0, tpb), mod.vec(0, 1, tpb),
                  pl.BlockSpec(w_bf.shape, lambda t: (0, 0)),
                  pl.BlockSpec(wkt_bf.shape, lambda t: (0, 0))],
        out_specs=[ospec, ospec_t, ospec, ospec, ospec],
        out_shape=[out, out_t, out, out, out],
        compiler_params=_cparams(("arbitrary",)),
        name="inproj",
    )(x2, g, mod.arr, mod.arr, w_bf, wkt_bf)


def _ctxproj_kernel(x_ref, g_ref, sh_ref, sc_ref, wkt_ref, wv_ref, kt_ref, v_ref):
    h = _norm_mod(x_ref[...], g_ref[...], sh_ref[...], sc_ref[...]).astype(BF16)
    kt_ref[...] = _dot_nt(wkt_ref[...], h).astype(BF16)
    v_ref[...] = _dot(h, wv_ref[...]).astype(BF16)


def _ctxproj(ctx2, g, mod, ctx_row, wkt_bf, wv_bf, l):
    n, d = ctx2.shape
    return pl.pallas_call(
        _ctxproj_kernel,
        grid=(n // l,),
        in_specs=[pl.BlockSpec((l, d), lambda t: (t, 0)), pl.BlockSpec((1, d), lambda t: (0, 0)),
                  mod.row(0, 0, ctx_row), mod.row(0, 1, ctx_row),
                  pl.BlockSpec(wkt_bf.shape, lambda t: (0, 0)),
                  pl.BlockSpec(wv_bf.shape, lambda t: (0, 0))],
        out_specs=[pl.BlockSpec((NA_WIDTH, l), lambda t: (0, t)),
                   pl.BlockSpec((l, NA_WIDTH), lambda t: (t, 0))],
        out_shape=[jax.ShapeDtypeStruct((NA_WIDTH, n), BF16), jax.ShapeDtypeStruct((n, NA_WIDTH), BF16)],
        compiler_params=_cparams(("arbitrary",)),
        name="ctxproj",
    )(ctx2, g, mod.arr, mod.arr, wkt_bf, wv_bf)


def _bias_row_index(rows):
    nb = rows // ATT_ROWS
    n_dr = 2 * NA_KR - 1
    dr_idx = np.full((3, ATT_ROWS, ATT_WIN), n_dr, np.int32)
    for ty, j in enumerate((0, 1, nb - 1)):
        ws = ATT_ROWS * j - NA_KR // 2
        for qr in range(ATT_ROWS):
            r = ATT_ROWS * j + qr
            rs = int(np.clip(r - NA_KR // 2, 0, rows - NA_KR))
            for kr in range(ATT_WIN):
                key_row = ws + kr
                if rs <= key_row < rs + NA_KR:
                    dr_idx[ty, qr, kr] = key_row - r + (NA_KR - 1)
    kr_blk = ATT_KBLK // GRID_W
    assert (dr_idx[:, :ATT_ROWS // 2, ATT_WIN - kr_blk:] == n_dr).all()
    assert (dr_idx[:, ATT_ROWS // 2:, :kr_blk] == n_dr).all()
    return dr_idx


def _bias_kernel(dr_idx, rpb_ref, out_ref, t_ref):
    h = pl.program_id(0)
    n_dr = 2 * NA_KR - 1
    n_dc = 2 * NA_KC - 1
    qc = lax.broadcasted_iota(I32, (GRID_W, GRID_W), 0)
    kc = lax.broadcasted_iota(I32, (GRID_W, GRID_W), 1)
    cs = jnp.clip(qc - NA_KC // 2, 0, GRID_W - NA_KC)
    valid = (kc >= cs) & (kc < cs + NA_KC)
    dc = jnp.where(valid, kc - qc + (NA_KC - 1), -1)
    for dr in range(n_dr):
        slab = jnp.full((GRID_W, GRID_W), NEG, F32)
        for m in range(n_dc):
            slab = jnp.where(dc == m, rpb_ref[(h * n_dr + dr) * n_dc + m] * LOG2E, slab)
        t_ref[dr] = slab
    t_ref[n_dr] = jnp.full((GRID_W, GRID_W), NEG, F32)
    for ty in range(3):
        for qr in range(ATT_ROWS):
            for kr in range(ATT_WIN):
                out_ref[ty, qr * GRID_W:(qr + 1) * GRID_W, kr * GRID_W:(kr + 1) * GRID_W] = \
                    t_ref[int(dr_idx[ty, qr, kr])]


def _attn_bias_tables(rpb, rows):
    h = rpb.shape[0]
    dr_idx = _bias_row_index(rows)
    qn, kn = ATT_ROWS * GRID_W, ATT_WIN * GRID_W
    return pl.pallas_call(
        functools.partial(_bias_kernel, dr_idx),
        grid=(h,),
        in_specs=[pl.BlockSpec(memory_space=pltpu.SMEM)],
        out_specs=pl.BlockSpec((None, 3, qn, kn), lambda i: (i, 0, 0, 0)),
        out_shape=jax.ShapeDtypeStruct((h, 3, qn, kn), F32),
        scratch_shapes=[pltpu.VMEM((2 * NA_KR, GRID_W, GRID_W), F32)],
        compiler_params=_cparams(("arbitrary",)),
        name="attn_bias_table",
    )(rpb.astype(F32).reshape(-1))


def _attn_kernel(q_ref, k0_ref, k1_ref, k2_ref, k3_ref, kc_ref,
                 v0_ref, v1_ref, v2_ref, v3_ref, vc_ref, tab_ref, o_ref):
    q = q_ref[...]
    lane = lax.broadcasted_iota(I32, (1, LANES), 1)
    kt = [k0_ref[...], k1_ref[...], k2_ref[...], k3_ref[...], kc_ref[...]]
    vb = [v0_ref[...], v1_ref[...], v2_ref[...], v3_ref[...], vc_ref[...]]
    kb = ATT_KBLK
    half = q.shape[0] // 2
    hms = [jnp.where((lane >= hh * NA_HEAD_DIM) & (lane < (hh + 1) * NA_HEAD_DIM), 1.0, 0.0).astype(BF16)
           for hh in range(2)]
    vh = [[v * hm for v in vb] for hm in hms]
    for r0, blocks in ((0, (0, 1, 2)), (half, (1, 2, 3))):
        acc = jnp.zeros((half, LANES), F32)
        for hh in range(2):
            qh = q[r0:r0 + half] * hms[hh]
            s = [_dot(qh, kt[i]) + tab_ref[hh, r0:r0 + half, i * kb:(i + 1) * kb] for i in blocks]
            s.append(_dot(qh, kt[4]))
            m = jnp.max(s[0], axis=-1, keepdims=True)
            for si in s[1:]:
                m = jnp.maximum(m, jnp.max(si, axis=-1, keepdims=True))
            p = [jnp.exp2(si - m) for si in s]
            l = jnp.sum(p[0], axis=-1, keepdims=True)
            for pi in p[1:]:
                l = l + jnp.sum(pi, axis=-1, keepdims=True)
            o = _dot(p[3].astype(BF16), vh[hh][4])
            for n_, i in enumerate(blocks):
                o = o + _dot(p[n_].astype(BF16), vh[hh][i])
            acc = acc + o / l
        o_ref[r0:r0 + half, :] = acc.astype(o_ref.dtype)


def _attention(q, kt, v, kct, vc, tab, b, s, l):
    n = q.shape[0]
    rows = s // GRID_W
    nb = rows // ATT_ROWS
    qblk = ATT_ROWS * GRID_W
    kpb = s // ATT_KBLK
    hp = NA_HEADS // 2

    kper = ATT_ROWS * GRID_W // ATT_KBLK
    koff = (NA_KR // 2) * GRID_W // ATT_KBLK

    def kblk(j, i):
        return jnp.clip(kper * j - koff + i, 0, kpb - 1)

    qspec = pl.BlockSpec((qblk, LANES), lambda h, j, bb: (bb * nb + j, h))
    ktspecs = [pl.BlockSpec((LANES, ATT_KBLK), lambda h, j, bb, i=i: (h, bb * kpb + kblk(j, i)))
               for i in range(4)]
    vspecs = [pl.BlockSpec((ATT_KBLK, LANES), lambda h, j, bb, i=i: (bb * kpb + kblk(j, i), h))
              for i in range(4)]
    ctspec = pl.BlockSpec((LANES, l), lambda h, j, bb: (h, bb))
    cspec = pl.BlockSpec((l, LANES), lambda h, j, bb: (bb, h))
    tspec = pl.BlockSpec((2, None, qblk, ATT_WIN * GRID_W),
                         lambda h, j, bb: (h, jnp.where(j == 0, 0, jnp.where(j == nb - 1, 2, 1)), 0, 0))
    return pl.pallas_call(
        _attn_kernel,
        grid=(hp, nb, b),
        in_specs=[qspec] + ktspecs + [ctspec] + vspecs + [cspec, tspec],
        out_specs=qspec,
        out_shape=jax.ShapeDtypeStruct((n, NA_WIDTH), BF16),
        compiler_params=_cparams(("arbitrary", "arbitrary", "arbitrary")),
        name="nbr_attention",
    )(q, kt, kt, kt, kt, kct, v, v, v, v, vc, tab)


def _residual_norm_route(x, y, g1, n2g, sh2, sc2, rwt, rb, xo_ref, ho_ref, ri_ref, rf_ref, cnt_ref):
    tm = x.shape[0]
    x1 = x + g1 * y
    xo_ref[...] = x1
    h2 = _norm_mod(x1, n2g, sh2, sc2)
    h2_hi, h2_lo = _split_bf16(h2)
    ho_ref[...] = h2_hi
    w_hi, w_lo = _split_bf16(rwt)
    lg = _dot_nt(jnp.concatenate([w_hi, w_lo], axis=0), h2_hi)
    logits = lg[:N_EXPERTS] + lg[N_EXPERTS:] + _dot_nt(w_hi, h2_lo)
    scores = _sigmoid(logits)
    sel = scores + rb
    r = [sel[e:e + 1, :] for e in range(N_EXPERTS)]
    sc = [scores[e:e + 1, :] for e in range(N_EXPERTS)]
    grp = []
    for g in range(N_GROUPS):
        a, b_, c, d = r[4 * g:4 * g + 4]
        hi1, lo1 = jnp.maximum(a, b_), jnp.minimum(a, b_)
        hi2, lo2 = jnp.maximum(c, d), jnp.minimum(c, d)
        m1 = jnp.maximum(hi1, hi2)
        m2 = jnp.maximum(jnp.minimum(hi1, hi2), jnp.maximum(lo1, lo2))
        grp.append(m1 + m2)
    best = grp[0]
    gi = jnp.zeros(best.shape, I32)
    for g in range(1, N_GROUPS):
        upd = grp[g] > best
        best = jnp.where(upd, grp[g], best)
        gi = jnp.where(upd, g, gi)
    chosen = []
    for e in range(N_EXPERTS):
        g = e // EXPERTS_PER_GROUP
        rank = jnp.zeros(best.shape, I32)
        for e2 in range(EXPERTS_PER_GROUP * g, EXPERTS_PER_GROUP * (g + 1)):
            if e2 == e:
                continue
            ahead = (r[e2] > r[e]) | ((r[e2] == r[e]) & (e2 < e))
            rank = rank + jnp.where(ahead, 1, 0)
        chosen.append((gi == g) & (rank < 2))
    wsel = [jnp.where(chosen[e], sc[e], 0.0) for e in range(N_EXPERTS)]
    denom = wsel[0]
    for e in range(1, N_EXPERTS):
        denom = denom + wsel[e]
    selm = jnp.concatenate([jnp.where(chosen[e], 1.0, 0.0) for e in range(N_EXPERTS)], axis=0)
    ii = lax.broadcasted_iota(I32, (tm, tm), 0)
    jj = lax.broadcasted_iota(I32, (tm, tm), 1)
    tri = jnp.where(ii < jj, 1.0, 0.0).astype(BF16)
    prefix = _dot(selm.astype(BF16), tri)
    cnt = jnp.sum(selm, axis=1, keepdims=True)
    cnt_ref[...] = jnp.broadcast_to(cnt, cnt_ref.shape).astype(I32)
    seen = jnp.zeros(best.shape, I32)
    e0 = jnp.full(best.shape, -1, I32)
    e1 = jnp.full(best.shape, -1, I32)
    r0 = jnp.full(best.shape, -4 * RCOMP, I32)
    r1 = jnp.full(best.shape, -4 * RCOMP, I32)
    g0 = jnp.zeros(best.shape, F32)
    g1_ = jnp.zeros(best.shape, F32)
    for e in range(N_EXPERTS):
        first = chosen[e] & (seen == 0)
        second = chosen[e] & (seen == 1)
        pe = prefix[e:e + 1, :].astype(I32)
        ge = wsel[e] / denom
        e0 = jnp.where(first, e, e0)
        e1 = jnp.where(second, e, e1)
        r0 = jnp.where(first, pe, r0)
        r1 = jnp.where(second, pe, r1)
        g0 = jnp.where(first, ge, g0)
        g1_ = jnp.where(second, ge, g1_)
        seen = seen + jnp.where(chosen[e], 1, 0)
    ri_ref[...] = jnp.zeros(ri_ref.shape, I32)
    rf_ref[...] = jnp.zeros(rf_ref.shape, F32)
    ri_ref[0:1, :] = e0
    ri_ref[1:2, :] = e1
    ri_ref[2:3, :] = r0
    ri_ref[3:4, :] = r1
    rf_ref[0:1, :] = g0
    rf_ref[1:2, :] = g1_


def _tail_specs(n, d, tm, mod, layer, tpb):
    assert tm == TMD
    one = pl.BlockSpec((1, d), lambda t: (0, 0))
    in_specs = [pl.BlockSpec((tm, d), lambda t: (t, 0)),
                mod.vec(layer, 2, tpb), one, mod.vec(layer, 3, tpb), mod.vec(layer, 4, tpb),
                pl.BlockSpec((N_EXPERTS, d), lambda t: (0, 0)),
                pl.BlockSpec((N_EXPERTS, 1), lambda t: (0, 0))]
    out_specs = [pl.BlockSpec((tm, d), lambda t: (t, 0)),
                 pl.BlockSpec((tm, d), lambda t: (t, 0)),
                 pl.BlockSpec((SUBLANES_F32, tm), lambda t: (0, t)),
                 pl.BlockSpec((SUBLANES_F32, tm), lambda t: (0, t)),
                 pl.BlockSpec((None, N_EXPERTS, LANES), lambda t: (t, 0, 0))]
    out_shape = [jax.ShapeDtypeStruct((n, d), F32), jax.ShapeDtypeStruct((n, d), BF16),
                 jax.ShapeDtypeStruct((SUBLANES_F32, n), I32), jax.ShapeDtypeStruct((SUBLANES_F32, n), F32),
                 jax.ShapeDtypeStruct((n // tm, N_EXPERTS, LANES), I32)]
    return in_specs, out_specs, out_shape


def _outproj_kernel(tpb, a_ref, gb_ref, u_ref, up_ref, un_ref, cw_ref, wa_ref, wb_ref,
                    x_ref, g1_ref, n2g_ref, sh2_ref, sc2_ref, rwt_ref, rb_ref,
                    xo_ref, ho_ref, ri_ref, rf_ref, cnt_ref):
    t = pl.program_id(0)
    tm = u_ref.shape[0]
    u = u_ref[...].astype(F32)
    keep_prev = jnp.where(t % tpb == 0, 0.0, 1.0)
    keep_next = jnp.where(t % tpb == tpb - 1, 0.0, 1.0)
    prev_row = up_ref[HALO - 1:HALO, :].astype(F32) * keep_prev
    next_row = un_ref[0:1, :].astype(F32) * keep_next
    row = lax.broadcasted_iota(I32, u.shape, 0)
    u_m1 = jnp.where(row == 0, prev_row, pltpu.roll(u, 1, 0))
    u_p1 = jnp.where(row == tm - 1, next_row, pltpu.roll(u, tm - 1, 0))
    conv = cw_ref[0:1, :] * u_m1 + cw_ref[1:2, :] * u + cw_ref[2:3, :] * u_p1
    bx = (gb_ref[...].astype(F32) * conv).astype(BF16)
    y = _dot(a_ref[...], wa_ref[...]) + _dot(bx, wb_ref[...])
    _residual_norm_route(x_ref[...], y, g1_ref[...], n2g_ref[...], sh2_ref[...], sc2_ref[...],
                         rwt_ref[...], rb_ref[...], xo_ref, ho_ref, ri_ref, rf_ref, cnt_ref)


def _outproj(a, gb, u, cw, wa, wb, x2, mod, n2g, rwt, rb, tpb):
    n, d = x2.shape
    tm = TMD
    w = NA_WIDTH
    hb = tm // HALO
    nh = n // HALO
    half = pl.BlockSpec((tm, w), lambda t: (t, 0))
    tail_in, out_specs, out_shape = _tail_specs(n, d, tm, mod, 0, tpb)
    in_specs = [half, half, half,
                pl.BlockSpec((HALO, w), lambda t: (jnp.maximum(t * hb - 1, 0), 0)),
                pl.BlockSpec((HALO, w), lambda t: (jnp.minimum((t + 1) * hb, nh - 1), 0)),
                pl.BlockSpec(cw.shape, lambda t: (0, 0)),
                pl.BlockSpec(wa.shape, lambda t: (0, 0)),
                pl.BlockSpec(wb.shape, lambda t: (0, 0))] + tail_in
    return pl.pallas_call(
        functools.partial(_outproj_kernel, tpb),
        grid=(n // tm,),
        in_specs=in_specs, out_specs=out_specs, out_shape=out_shape,
        compiler_params=_cparams(("arbitrary",)),
        name="outproj_route",
    )(a, gb, u, u, u, cw, wa, wb, x2, mod.arr, n2g, mod.arr, mod.arr, rwt, rb)


def _pw1_kernel(x_ref, g_ref, sh_ref, sc_ref, w_ref, b_ref, u_ref):
    d = x_ref.shape[1]
    h = _norm_mod(x_ref[...], g_ref[...], sh_ref[...], sc_ref[...]).astype(BF16)
    a = _dot(h, w_ref[:, 0:d]) + b_ref[:, 0:d]
    g = _dot(h, w_ref[:, d:2 * d]) + b_ref[:, d:2 * d]
    u_ref[...] = (a * _sigmoid(g)).astype(BF16)


def _pw1(x2, g, mod, w_bf, b, tpb):
    n, d = x2.shape
    tm = TM_PROJ
    return pl.pallas_call(
        _pw1_kernel,
        grid=(n // tm,),
        in_specs=[pl.BlockSpec((tm, d), lambda t: (t, 0)),
                  pl.BlockSpec((1, d), lambda t: (0, 0)), mod.vec(1, 0, tpb), mod.vec(1, 1, tpb),
                  pl.BlockSpec(w_bf.shape, lambda t: (0, 0)),
                  pl.BlockSpec(b.shape, lambda t: (0, 0))],
        out_specs=pl.BlockSpec((tm, d), lambda t: (t, 0)),
        out_shape=jax.ShapeDtypeStruct((n, d), BF16),
        compiler_params=_cparams(("arbitrary",)),
        name="conf_pw1_glu",
    )(x2, g, mod.arr, mod.arr, w_bf, b)


def _conf_kernel(tpb, u_ref, up_ref, un_ref, dww_ref, dwb_ref, lng_ref, lnb_ref, w2_ref, b2_ref,
                 x_ref, g1_ref, n2g_ref, sh2_ref, sc2_ref, rwt_ref, rb_ref,
                 xo_ref, ho_ref, ri_ref, rf_ref, cnt_ref, ue_ref, sh_ref, conv_ref, wb_ref):
    t = pl.program_id(0)
    tm, d = u_ref.shape
    taps = dww_ref.shape[0]
    sl = SUBLANES_F32

    @pl.when(t == 0)
    def _():
        for k in range(taps):
            wb_ref[k * sl:(k + 1) * sl, :] = jnp.broadcast_to(dww_ref[k:k + 1, :], (sl, d))

    keep_prev = jnp.where(t % tpb == 0, 0.0, 1.0)
    keep_next = jnp.where(t % tpb == tpb - 1, 0.0, 1.0)
    ue_ref[0:HALO, :] = up_ref[...].astype(F32) * keep_prev
    ue_ref[HALO:HALO + tm, :] = u_ref[...].astype(F32)
    ue_ref[HALO + tm:HALO + tm + HALO, :] = un_ref[...].astype(F32) * keep_next
    ext = tm + 2 * HALO - sl
    for s in range(sl):
        sh_ref[s, :, :] = ue_ref[s:s + ext, :]
    off0 = HALO - (taps - 1) // 2
    nv = CONV_RC // sl
    for cb in range(d // LANES):
        cols = slice(cb * LANES, (cb + 1) * LANES)

        def body(rc, carry, cols=cols):
            r0 = pl.multiple_of(rc * CONV_RC, CONV_RC)
            accs = [None] * nv
            for k in range(taps):
                s, a = (k + off0) % sl, (k + off0) // sl
                wk = wb_ref[k * sl:(k + 1) * sl, cols]
                for j in range(nv):
                    term = wk * sh_ref[s, pl.ds(r0 + sl * (a + j), sl), cols]
                    accs[j] = term if accs[j] is None else accs[j] + term
            for j in range(nv):
                conv_ref[pl.ds(r0 + sl * j, sl), cols] = accs[j]
            return carry

        lax.fori_loop(0, tm // CONV_RC, body, 0)
    c = conv_ref[...] + dwb_ref[...]
    mu = jnp.mean(c, axis=-1, keepdims=True)
    cc = c - mu
    var = jnp.mean(cc * cc, axis=-1, keepdims=True)
    z = cc * lax.rsqrt(var + EPS) * lng_ref[...] + lnb_ref[...]
    z = (z * _sigmoid(z)).astype(BF16)
    y = _dot(z, w2_ref[...]) + b2_ref[...]
    _residual_norm_route(x_ref[...], y, g1_ref[...], n2g_ref[...], sh2_ref[...], sc2_ref[...],
                         rwt_ref[...], rb_ref[...], xo_ref, ho_ref, ri_ref, rf_ref, cnt_ref)


def _conf(u, dww, dwb, lng, lnb, w2, b2, x2, mod, n2g, rwt, rb, tpb):
    n, d = x2.shape
    tm = TMD
    hb = tm // HALO
    nh = n // HALO
    taps = dww.shape[0]
    assert (taps - 1) // 2 <= HALO - 1 and taps // 2 <= HALO
    one = pl.BlockSpec((1, d), lambda t: (0, 0))
    tail_in, out_specs, out_shape = _tail_specs(n, d, tm, mod, 1, tpb)
    in_specs = [pl.BlockSpec((tm, d), lambda t: (t, 0)),
                pl.BlockSpec((HALO, d), lambda t: (jnp.maximum(t * hb - 1, 0), 0)),
                pl.BlockSpec((HALO, d), lambda t: (jnp.minimum((t + 1) * hb, nh - 1), 0)),
                pl.BlockSpec(dww.shape, lambda t: (0, 0)), one, one, one,
                pl.BlockSpec(w2.shape, lambda t: (0, 0)), one] + tail_in
    ext = tm + 2 * HALO - SUBLANES_F32
    return pl.pallas_call(
        functools.partial(_conf_kernel, tpb),
        grid=(n // tm,),
        in_specs=in_specs, out_specs=out_specs, out_shape=out_shape,
        scratch_shapes=[pltpu.VMEM((tm + 2 * HALO, d), F32),
                        pltpu.VMEM((SUBLANES_F32, ext, d), F32),
                        pltpu.VMEM((tm, d), F32),
                        pltpu.VMEM((taps * SUBLANES_F32, d), F32)],
        compiler_params=_cparams(("arbitrary",)),
        name="conf_conv_route",
    )(u, u, u, dww, dwb, lng, lnb, w2, b2, x2, mod.arr, n2g, mod.arr, mod.arr, rwt, rb)


def _sorted_rows(n):
    t_n = n // TMD
    rows = 2 * n + (CHUNK - 1) * t_n * N_EXPERTS + N_EXPERTS * (TME - CHUNK)
    return (rows + TME - 1) // TME * TME


def _moe_plan(cnt, n):
    t_n, e_n = cnt.shape
    rt = _sorted_rows(n)
    seg = (cnt + CHUNK - 1) // CHUNK * CHUNK
    lo = jnp.cumsum(seg, axis=1) - seg
    etot = jnp.sum(seg, axis=0)
    epad = (etot + TME - 1) // TME * TME
    eend = jnp.cumsum(epad)
    estart = eend - epad
    go = estart[None, :] + jnp.cumsum(seg, axis=0) - seg
    nchunk = jnp.sum(seg, axis=1) // CHUNK
    c_rows = jnp.arange(CMAX, dtype=I32) * CHUNK
    e_ids = jnp.arange(e_n, dtype=I32)
    owner = jnp.sum(((lo + seg)[:, None, :] <= c_rows[None, :, None]).astype(I32), axis=-1)
    owner = jnp.minimum(owner, e_n - 1)
    shift = jnp.sum(jnp.where(owner[:, :, None] == e_ids, (go - lo)[:, None, :], 0), axis=-1)
    cmap = jnp.clip((shift + c_rows[None, :]) // CHUNK, 0, rt // CHUNK - 1)
    npad = (epad - etot) // CHUNK
    pend = jnp.cumsum(npad)
    kk = jnp.arange(e_n * (TME // CHUNK - 1), dtype=I32)
    pown = jnp.minimum(jnp.sum((pend[None, :] <= kk[:, None]).astype(I32), axis=-1), e_n - 1)
    pbase = (estart + etot) // CHUNK - (pend - npad)
    padmap = kk + jnp.sum(jnp.where(pown[:, None] == e_ids, pbase[None, :], 0), axis=-1)
    padmap = jnp.clip(padmap, 0, rt // CHUNK - 1)
    tile_row = jnp.arange(rt // TME, dtype=I32) * TME
    texp = jnp.minimum(jnp.sum((eend[None, :] <= tile_row[:, None]).astype(I32), axis=-1), e_n - 1)
    as_i32 = lambda a: a.astype(I32)
    return dict(lo=as_i32(lo.reshape(-1)), nchunk=as_i32(nchunk), cmap=as_i32(cmap.reshape(-1)),
                padmap=as_i32(padmap), npadtot=as_i32(pend[-1:]), texp=as_i32(texp),
                nact=as_i32(eend[-1:] // TME))


def _chunk_copy(src_ref, dst_ref, sem):
    return pltpu.make_async_copy(src_ref, dst_ref, sem)


def _wait_chunks(n_chunks, desc):
    for bit in range(CMAX.bit_length()):
        @pl.when(((n_chunks >> bit) & 1) == 1)
        def _(bit=bit):
            desc(CHUNK << bit).wait()


def _dispatch_kernel(nchunk_ref, cmap_ref, npad_ref, padmap_ref, nact_ref, lo_ref,
                     h_ref, ri_ref, xs_ref, lp_ref, xc_ref, z_ref, sem, zsem):
    t = pl.program_id(0)
    nt = pl.num_programs(0)
    slot = t % 2
    ntail = xs_ref.shape[0] // TME - nact_ref[0]

    def pad_copy(k):
        g = pl.multiple_of(padmap_ref[k] * CHUNK, CHUNK)
        return _chunk_copy(z_ref.at[0:CHUNK, :], xs_ref.at[pl.ds(g, CHUNK), :], zsem.at[0])

    def tail_copy(k):
        g = pl.multiple_of((nact_ref[0] + k) * TME, TME)
        return _chunk_copy(z_ref, xs_ref.at[pl.ds(g, TME), :], zsem.at[0])

    @pl.when(t == 0)
    def _():
        z_ref[...] = jnp.zeros(z_ref.shape, z_ref.dtype)
        lax.fori_loop(0, npad_ref[0], lambda k, c: (pad_copy(k).start(), c)[1], 0)
        lax.fori_loop(0, ntail, lambda k, c: (tail_copy(k).start(), c)[1], 0)

    e0, e1 = ri_ref[0:1, :], ri_ref[1:2, :]
    b0 = jnp.zeros(e0.shape, I32)
    b1 = jnp.zeros(e0.shape, I32)
    for e in range(N_EXPERTS):
        lo_e = lo_ref[t * N_EXPERTS + e]
        b0 = jnp.where(e0 == e, lo_e, b0)
        b1 = jnp.where(e1 == e, lo_e, b1)
    lp0 = b0 + ri_ref[2:3, :]
    lp1 = b1 + ri_ref[3:4, :]
    lp_ref[...] = jnp.zeros(lp_ref.shape, I32)
    lp_ref[0:1, :] = lp0
    lp_ref[1:2, :] = lp1

    rows = lax.broadcasted_iota(I32, (RCOMP, TMD), 0)
    onehot = jnp.where((lp0 == rows) | (lp1 == rows), 1.0, 0.0).astype(BF16)
    xc_ref[slot] = _dot(onehot, h_ref[...]).astype(BF16)

    def out_copy(tt, sl, c):
        g = pl.multiple_of(cmap_ref[tt * CMAX + c] * CHUNK, CHUNK)
        l = pl.multiple_of(c * CHUNK, CHUNK)
        return _chunk_copy(xc_ref.at[sl, pl.ds(l, CHUNK), :], xs_ref.at[pl.ds(g, CHUNK), :], sem.at[sl])

    lax.fori_loop(0, nchunk_ref[t], lambda c, cr: (out_copy(t, slot, c).start(), cr)[1], 0)

    def out_desc(sl):
        return lambda rows: _chunk_copy(xc_ref.at[sl, 0:rows, :], xs_ref.at[0:rows, :], sem.at[sl])

    @pl.when(t > 0)
    def _():
        _wait_chunks(nchunk_ref[t - 1], out_desc(1 - slot))

    @pl.when(t == 0)
    def _():
        lax.fori_loop(0, npad_ref[0], lambda k, c: (pad_copy(k).wait(), c)[1], 0)
        lax.fori_loop(0, ntail, lambda k, c: (tail_copy(k).wait(), c)[1], 0)

    @pl.when(t == nt - 1)
    def _():
        _wait_chunks(nchunk_ref[t], out_desc(slot))


def _dispatch(h2, ri, plan):
    n, d = h2.shape
    t_n = n // TMD
    rt = _sorted_rows(n)
    grid_spec = pltpu.PrefetchScalarGridSpec(
        num_scalar_prefetch=6,
        grid=(t_n,),
        in_specs=[pl.BlockSpec((TMD, d), lambda t, *_: (t, 0)),
                  pl.BlockSpec((SUBLANES_F32, TMD), lambda t, *_: (0, t))],
        out_specs=[pl.BlockSpec(memory_space=pl.ANY),
                   pl.BlockSpec((None, SUBLANES_F32, TMD), lambda t, *_: (t, 0, 0))],
        scratch_shapes=[pltpu.VMEM((2, RCOMP, d), BF16),
                        pltpu.VMEM((TME, d), BF16),
                        pltpu.SemaphoreType.DMA((2,)),
                        pltpu.SemaphoreType.DMA((1,))],
    )
    return pl.pallas_call(
        _dispatch_kernel,
        grid_spec=grid_spec,
        out_shape=[jax.ShapeDtypeStruct((rt, d), BF16),
                   jax.ShapeDtypeStruct((t_n, SUBLANES_F32, TMD), I32)],
        compiler_params=_cparams(("arbitrary",)),
        name="moe_dispatch",
    )(plan["nchunk"], plan["cmap"], plan["npadtot"], plan["padmap"], plan["nact"], plan["lo"], h2, ri)


def _expert_kernel(texp_ref, nact_ref, x_ref, w1_ref, w3_ref, w2_ref, y_ref, w1b, w3b, w2b):
    i = pl.program_id(0)

    @pl.when(i < nact_ref[0])
    def _():
        prev = texp_ref[jnp.maximum(i - 1, 0)]

        @pl.when((i == 0) | (texp_ref[i] != prev))
        def _():
            w1b[...] = w1_ref[...].astype(BF16)
            w3b[...] = w3_ref[...].astype(BF16)
            w2b[...] = w2_ref[...].astype(BF16)

        x = x_ref[...]
        h = _dot(x, w1b[...])
        g = _dot(x, w3b[...])
        a = (h * _sigmoid(h) * g).astype(BF16)
        y_ref[...] = _dot(a, w2b[...]).astype(y_ref.dtype)

    @pl.when(i >= nact_ref[0])
    def _():
        y_ref[...] = jnp.zeros(y_ref.shape, y_ref.dtype)


def _experts(xs, w1, w3, w2, layer, plan):
    rt, d = xs.shape
    f = w1.shape[-1]
    nt = rt // TME

    def row_map(i, texp, nact):
        return (jnp.clip(i, 0, jnp.maximum(nact[0] - 1, 0)), 0)

    def w_map(i, texp, nact):
        return (layer, texp[jnp.clip(i, 0, jnp.maximum(nact[0] - 1, 0))], 0, 0)

    grid_spec = pltpu.PrefetchScalarGridSpec(
        num_scalar_prefetch=2,
        grid=(nt,),
        in_specs=[pl.BlockSpec((TME, d), row_map),
                  pl.BlockSpec((None, None, d, f), w_map),
                  pl.BlockSpec((None, None, d, f), w_map),
                  pl.BlockSpec((None, None, f, d), w_map)],
        out_specs=pl.BlockSpec((TME, d), lambda i, texp, nact: (i, 0)),
        scratch_shapes=[pltpu.VMEM((d, f), BF16), pltpu.VMEM((d, f), BF16), pltpu.VMEM((f, d), BF16)],
    )
    return pl.pallas_call(
        _expert_kernel,
        grid_spec=grid_spec,
        out_shape=jax.ShapeDtypeStruct((rt, d), BF16),
        compiler_params=_cparams(("arbitrary",)),
        name="moe_experts",
    )(plan["texp"], plan["nact"], xs, w1, w3, w2)


def _combine_kernel(final, nchunk_ref, cmap_ref, ys_ref, lp0_ref, lp1_ref, g0_ref, g1_ref,
                    x_ref, g2_ref, fg_ref, o_ref, yc_ref, sem):
    t = pl.program_id(0)
    nt = pl.num_programs(0)
    slot = t % 2

    def in_copy(tt, sl, c):
        g = pl.multiple_of(cmap_ref[tt * CMAX + c] * CHUNK, CHUNK)
        l = pl.multiple_of(c * CHUNK, CHUNK)
        return _chunk_copy(ys_ref.at[pl.ds(g, CHUNK), :], yc_ref.at[sl, pl.ds(l, CHUNK), :], sem.at[sl])

    @pl.when(t == 0)
    def _():
        yc_ref[...] = jnp.zeros(yc_ref.shape, yc_ref.dtype)
        lax.fori_loop(0, nchunk_ref[0], lambda c, cr: (in_copy(0, 0, c).start(), cr)[1], 0)

    @pl.when(t + 1 < nt)
    def _():
        lax.fori_loop(0, nchunk_ref[t + 1], lambda c, cr: (in_copy(t + 1, 1 - slot, c).start(), cr)[1], 0)

    _wait_chunks(nchunk_ref[t],
                 lambda rows: _chunk_copy(ys_ref.at[0:rows, :], yc_ref.at[slot, 0:rows, :], sem.at[slot]))

    cols = lax.broadcasted_iota(I32, (TMD, RCOMP), 1)
    w = jnp.where(lp0_ref[...] == cols, g0_ref[...], 0.0) + jnp.where(lp1_ref[...] == cols, g1_ref[...], 0.0)
    moe = _dot(w.astype(BF16), yc_ref[slot])
    x1 = x_ref[...] + g2_ref[...] * moe
    if final:
        ms = jnp.mean(x1 * x1, axis=-1, keepdims=True)
        x1 = x1 * lax.rsqrt(ms + EPS) * fg_ref[...]
    o_ref[...] = x1


def _combine(ys, plan, lp, rf, x2, mod, layer, fg, tpb, final):
    n, d = x2.shape
    t_n = n // TMD
    col = pl.BlockSpec((TMD, 1), lambda t, *_: (t, 0))
    grid_spec = pltpu.PrefetchScalarGridSpec(
        num_scalar_prefetch=2,
        grid=(t_n,),
        in_specs=[pl.BlockSpec(memory_space=pl.ANY), col, col, col, col,
                  pl.BlockSpec((TMD, d), lambda t, *_: (t, 0)),
                  mod.vec(layer, 5, tpb),
                  pl.BlockSpec((1, d), lambda t, *_: (0, 0))],
        out_specs=pl.BlockSpec((TMD, d), lambda t, *_: (t, 0)),
        scratch_shapes=[pltpu.VMEM((2, RCOMP, d), BF16), pltpu.SemaphoreType.DMA((2,))],
    )
    return pl.pallas_call(
        functools.partial(_combine_kernel, final),
        grid_spec=grid_spec,
        out_shape=jax.ShapeDtypeStruct((n, d), F32),
        compiler_params=_cparams(("arbitrary",)),
        name="moe_combine_final" if final else "moe_combine",
    )(plan["nchunk"], plan["cmap"], ys,
      lp[:, 0, :].reshape(n, 1), lp[:, 1, :].reshape(n, 1), rf[0].reshape(n, 1), rf[1].reshape(n, 1),
      x2, mod.arr, fg)


def _moe(x1, h2, ri, rf, cnt, w1, w3, w2, layer, mod, fg, tpb, final):
    n = x1.shape[0]
    plan = _moe_plan(cnt[:, :, 0], n)
    xs, lp = _dispatch(h2, ri, plan)
    ys = _experts(xs, w1, w3, w2, layer, plan)
    return _combine(ys, plan, lp, rf, x1, mod, layer, fg, tpb, final)


def kernel(x, c, ctx, c_ctx, norm1_g, norm2_g, w_mod, b_mod, na_w_in, na_rpb, sc_conv_w, ab_w_out,
           cf_pw1_w, cf_pw1_b, cf_dw_w, cf_dw_b, cf_ln_g, cf_ln_b, cf_pw2_w, cf_pw2_b,
           router_w, router_b, moe_w1, moe_w3, moe_w2, final_g):
    b, s, d = x.shape
    l = ctx.shape[1]
    n = b * s
    depth = w_mod.shape[0]
    assert depth == 2, "layer pattern (attention layer, Conformer layer) is written out for depth 2"
    assert s % TM_PROJ == 0 and s % TMD == 0
    rows = s // GRID_W
    assert rows % ATT_ROWS == 0 and rows >= ATT_WIN and s % ATT_KBLK == 0
    tpb = s // TM_PROJ
    tpb_d = s // TMD

    mr = (b + 1 + SUBLANES_F32 - 1) // SUBLANES_F32 * SUBLANES_F32
    c_all = jnp.concatenate([c, c_ctx[None, :], jnp.zeros((mr - b - 1, d), F32)], axis=0)
    mod = _Mod(_modulation(c_all, w_mod, b_mod), d)

    x2 = x.reshape(n, d)
    rwt = router_w.T
    rb = router_b.reshape(N_EXPERTS, 1)
    fg = final_g.reshape(1, d)

    w_in = na_w_in[0].astype(BF16)
    n1g = norm1_g[0].reshape(1, d)
    wkt = w_in[:, NA_WIDTH:2 * NA_WIDTH].T
    q, kt, v, gb, u = _inproj(x2, n1g, mod, w_in, wkt, tpb)
    kct, vc = _ctxproj(ctx.reshape(b * l, d), n1g, mod, b, wkt, w_in[:, 2 * NA_WIDTH:3 * NA_WIDTH], l)
    tab = _attn_bias_tables(na_rpb[0], rows)
    a = _attention(q, kt, v, kct, vc, tab, b, s, l)
    w_out = ab_w_out[0].astype(BF16)
    x1, h2, ri, rf, cnt = _outproj(a, gb, u, sc_conv_w[0], w_out[:NA_WIDTH], w_out[NA_WIDTH:], x2,
                                   mod, norm2_g[0].reshape(1, d), rwt, rb, tpb_d)
    x2 = _moe(x1, h2, ri, rf, cnt, moe_w1, moe_w3, moe_w2, 0, mod, fg, tpb_d, False)

    u = _pw1(x2, norm1_g[1].reshape(1, d), mod, cf_pw1_w[0].astype(BF16),
             cf_pw1_b[0].reshape(1, 2 * d), tpb)
    x1, h2, ri, rf, cnt = _conf(u, cf_dw_w[0], cf_dw_b[0].reshape(1, d), cf_ln_g[0].reshape(1, d),
                                cf_ln_b[0].reshape(1, d), cf_pw2_w[0].astype(BF16),
                                cf_pw2_b[0].reshape(1, d), x2, mod, norm2_g[1].reshape(1, d),
                                rwt, rb, tpb_d)
    out = _moe(x1, h2, ri, rf, cnt, moe_w1, moe_w3, moe_w2, 1, mod, fg, tpb_d, True)
    return out.reshape(b, s, d)
```

```python
import functools

import numpy as np
import jax
import jax.numpy as jnp
from jax import lax
from jax.experimental import pallas as pl
from jax.experimental.pallas import tpu as pltpu

F32 = jnp.float32
BF16 = jnp.bfloat16
I32 = jnp.int32

GRID_W = 64
NA_HEADS = 8
NA_HEAD_DIM = 64
NA_WIDTH = NA_HEADS * NA_HEAD_DIM
NA_KR = 8
NA_KC = 16
N_EXPERTS = 16
N_GROUPS = 4
EXPERTS_PER_GROUP = N_EXPERTS // N_GROUPS
N_MOD = 6
EPS = 1e-6
NEG = -1e30
LOG2E = 1.4426950408889634

LANES = 128
SUBLANES_F32 = 8
SUBLANES_BF16 = 16

TM_PROJ = 512
ATT_ROWS = 8
ATT_WIN = 2 * ATT_ROWS
ATT_KBLK = 256
ATT_HEADS = 4
TMD = 512
CHUNK = SUBLANES_BF16
CMAX = (2 * TMD + (N_EXPERTS - 1) * CHUNK + CHUNK - 1) // CHUNK + 1
RCOMP = CMAX * CHUNK
TME = 1024
CONV_RC = 128
HALO = SUBLANES_BF16
VMEM_LIMIT = 52 * 1024 * 1024


def _cparams(sem):
    return pltpu.CompilerParams(dimension_semantics=sem, vmem_limit_bytes=VMEM_LIMIT)


def _dot(a, b):
    return jnp.dot(a, b, preferred_element_type=F32)


def _dot_nt(a, b, precision=None):
    return lax.dot_general(a, b, (((1,), (1,)), ((), ())), precision=precision,
                           preferred_element_type=F32)


def _sigmoid(x):
    return 1.0 / (1.0 + jnp.exp(-x))


def _norm_mod(x, g, sh, sc):
    ms = jnp.mean(x * x, axis=-1, keepdims=True)
    y = x * lax.rsqrt(ms + EPS) * g
    return y * (1.0 + sc) + sh


def _split_bf16(a):
    hi = a.astype(BF16)
    lo = (a - hi.astype(F32)).astype(BF16)
    return hi, lo


def _mod_kernel(c_ref, w_ref, b_ref, o_ref):
    c = c_ref[...]
    mr = c.shape[0]
    hi, lo = _split_bf16(c * _sigmoid(c))
    w_hi, w_lo = _split_bf16(w_ref[...])
    r = _dot(jnp.concatenate([hi, lo], axis=0), w_hi)
    o_ref[...] = r[:mr] + r[mr:] + _dot(hi, w_lo) + b_ref[...]


def _modulation(c_all, w_mod, b_mod):
    depth, d, n6 = w_mod.shape
    mr = c_all.shape[0]
    tn = 1024
    return pl.pallas_call(
        _mod_kernel,
        grid=(depth, n6 // tn),
        in_specs=[pl.BlockSpec((mr, d), lambda i, j: (0, 0)),
                  pl.BlockSpec((None, d, tn), lambda i, j: (i, 0, j)),
                  pl.BlockSpec((None, 1, tn), lambda i, j: (i, 0, j))],
        out_specs=pl.BlockSpec((None, mr, tn), lambda i, j: (i, 0, j)),
        out_shape=jax.ShapeDtypeStruct((depth, mr, n6), F32),
        compiler_params=_cparams(("arbitrary", "arbitrary")),
        name="modulation",
    )(c_all, w_mod, b_mod.reshape(depth, 1, n6))


class _Mod:
    def __init__(self, mod, d):
        self.depth, self.mr, _ = mod.shape
        self.d = d
        self.arr = mod.reshape(self.depth * self.mr * N_MOD, 1, d)

    def vec(self, layer, j, tpb):
        mr = self.mr
        return pl.BlockSpec((None, 1, self.d),
                            lambda t, *_: ((layer * mr + t // tpb) * N_MOD + j, 0, 0))

    def row(self, layer, j, r):
        mr = self.mr
        return pl.BlockSpec((None, 1, self.d), lambda t, *_: ((layer * mr + r) * N_MOD + j, 0, 0))


def _inproj_kernel(x_ref, g_ref, sh_ref, sc_ref, w_ref, wkt_ref, q_ref, kt_ref, v_ref, gb_ref, u_ref):
    h = _norm_mod(x_ref[...], g_ref[...], sh_ref[...], sc_ref[...]).astype(BF16)
    w = NA_WIDTH
    q_ref[...] = (_dot(h, w_ref[:, 0:w]) * (NA_HEAD_DIM ** -0.5 * LOG2E)).astype(BF16)
    kt_ref[...] = _dot_nt(wkt_ref[...], h).astype(BF16)
    v_ref[...] = _dot(h, w_ref[:, 2 * w:3 * w]).astype(BF16)
    gb_ref[...] = _dot(h, w_ref[:, 3 * w:4 * w]).astype(BF16)
    u_ref[...] = (_dot(h, w_ref[:, 4 * w:5 * w]) * _dot(h, w_ref[:, 5 * w:6 * w])).astype(BF16)


def _inproj(x2, g, mod, w_bf, wkt_bf, tpb):
    n, d = x2.shape
    tm = TM_PROJ
    out = jax.ShapeDtypeStruct((n, NA_WIDTH), BF16)
    ospec = pl.BlockSpec((tm, NA_WIDTH), lambda t: (t, 0))
    out_t = jax.ShapeDtypeStruct((NA_WIDTH, n), BF16)
    ospec_t = pl.BlockSpec((NA_WIDTH, tm), lambda t: (0, t))
    return pl.pallas_call(
        _inproj_kernel,
        grid=(n // tm,),
        in_specs=[pl.BlockSpec((tm, d), lambda t: (t, 0)),
                  pl.BlockSpec((1, d), lambda t: (0, 0)), mod.vec(0, 0, tpb), mod.vec(0, 1, tpb),
                  pl.BlockSpec(w_bf.shape, lambda t: (0, 0)),
                  pl.BlockSpec(wkt_bf.shape, lambda t: (0, 0))],
        out_specs=[ospec, ospec_t, ospec, ospec, ospec],
        out_shape=[out, out_t, out, out, out],
        compiler_params=_cparams(("arbitrary",)),
        name="inproj",
    )(x2, g, mod.arr, mod.arr, w_bf, wkt_bf)


def _ctxproj_kernel(x_ref, g_ref, sh_ref, sc_ref, wkt_ref, wv_ref, kt_ref, v_ref):
    h = _norm_mod(x_ref[...], g_ref[...], sh_ref[...], sc_ref[...]).astype(BF16)
    kt_ref[...] = _dot_nt(wkt_ref[...], h).astype(BF16)
    v_ref[...] = _dot(h, wv_ref[...]).astype(BF16)


def _ctxproj(ctx2, g, mod, ctx_row, wkt_bf, wv_bf, l):
    n, d = ctx2.shape
    return pl.pallas_call(
        _ctxproj_kernel,
        grid=(n // l,),
        in_specs=[pl.BlockSpec((l, d), lambda t: (t, 0)), pl.BlockSpec((1, d), lambda t: (0, 0)),
                  mod.row(0, 0, ctx_row), mod.row(0, 1, ctx_row),
                  pl.BlockSpec(wkt_bf.shape, lambda t: (0, 0)),
                  pl.BlockSpec(wv_bf.shape, lambda t: (0, 0))],
        out_specs=[pl.BlockSpec((NA_WIDTH, l), lambda t: (0, t)),
                   pl.BlockSpec((l, NA_WIDTH), lambda t: (t, 0))],
        out_shape=[jax.ShapeDtypeStruct((NA_WIDTH, n), BF16), jax.ShapeDtypeStruct((n, NA_WIDTH), BF16)],
        compiler_params=_cparams(("arbitrary",)),
        name="ctxproj",
    )(ctx2, g, mod.arr, mod.arr, wkt_bf, wv_bf)


def _bias_row_index(rows):
    nb = rows // ATT_ROWS
    n_dr = 2 * NA_KR - 1
    dr_idx = np.full((3, ATT_ROWS, ATT_WIN), n_dr, np.int32)
    for ty, j in enumerate((0, 1, nb - 1)):
        ws = ATT_ROWS * j - NA_KR // 2
        for qr in range(ATT_ROWS):
            r = ATT_ROWS * j + qr
            rs = int(np.clip(r - NA_KR // 2, 0, rows - NA_KR))
            for kr in range(ATT_WIN):
                key_row = ws + kr
                if rs <= key_row < rs + NA_KR:
                    dr_idx[ty, qr, kr] = key_row - r + (NA_KR - 1)
    kr_blk = ATT_KBLK // GRID_W
    assert (dr_idx[:, :ATT_ROWS // 2, ATT_WIN - kr_blk:] == n_dr).all()
    assert (dr_idx[:, ATT_ROWS // 2:, :kr_blk] == n_dr).all()
    return dr_idx


def _bias_kernel(dr_idx, rpb_ref, out_ref, t_ref):
    h = pl.program_id(0)
    n_dr = 2 * NA_KR - 1
    n_dc = 2 * NA_KC - 1
    qc = lax.broadcasted_iota(I32, (GRID_W, GRID_W), 0)
    kc = lax.broadcasted_iota(I32, (GRID_W, GRID_W), 1)
    cs = jnp.clip(qc - NA_KC // 2, 0, GRID_W - NA_KC)
    valid = (kc >= cs) & (kc < cs + NA_KC)
    dc = jnp.where(valid, kc - qc + (NA_KC - 1), -1)
    for dr in range(n_dr):
        slab = jnp.full((GRID_W, GRID_W), NEG, F32)
        for m in range(n_dc):
            slab = jnp.where(dc == m, rpb_ref[(h * n_dr + dr) * n_dc + m] * LOG2E, slab)
        t_ref[dr] = slab
    t_ref[n_dr] = jnp.full((GRID_W, GRID_W), NEG, F32)
    for ty in range(3):
        for qr in range(ATT_ROWS):
            for kr in range(ATT_WIN):
                out_ref[ty, qr * GRID_W:(qr + 1) * GRID_W, kr * GRID_W:(kr + 1) * GRID_W] = \
                    t_ref[int(dr_idx[ty, qr, kr])]


def _attn_bias_tables(rpb, rows):
    h = rpb.shape[0]
    dr_idx = _bias_row_index(rows)
    qn, kn = ATT_ROWS * GRID_W, ATT_WIN * GRID_W
    return pl.pallas_call(
        functools.partial(_bias_kernel, dr_idx),
        grid=(h,),
        in_specs=[pl.BlockSpec(memory_space=pltpu.SMEM)],
        out_specs=pl.BlockSpec((None, 3, qn, kn), lambda i: (i, 0, 0, 0)),
        out_shape=jax.ShapeDtypeStruct((h, 3, qn, kn), F32),
        scratch_shapes=[pltpu.VMEM((2 * NA_KR, GRID_W, GRID_W), F32)],
        compiler_params=_cparams(("arbitrary",)),
        name="attn_bias_table",
    )(rpb.astype(F32).reshape(-1))


def _attn_kernel(q_ref, k0_ref, k1_ref, k2_ref, k3_ref, kc_ref,
                 v0_ref, v1_ref, v2_ref, v3_ref, vc_ref, tab_ref, o_ref):
    lane = lax.broadcasted_iota(I32, (1, LANES), 1)
    kt_refs = [k0_ref, k1_ref, k2_ref, k3_ref, kc_ref]
    v_refs = [v0_ref, v1_ref, v2_ref, v3_ref, vc_ref]
    kb = ATT_KBLK
    half = q_ref.shape[0] // 2
    hms = [jnp.where((lane >= hh * NA_HEAD_DIM) & (lane < (hh + 1) * NA_HEAD_DIM), 1.0, 0.0).astype(BF16)
           for hh in range(2)]
    for pp in range(ATT_HEADS // 2):
        cols = slice(pp * LANES, (pp + 1) * LANES)
        kt = [r[cols, :] for r in kt_refs]
        vh = [[r[:, cols] * hm for r in v_refs] for hm in hms]
        for r0, blocks in ((0, (0, 1, 2)), (half, (1, 2, 3))):
            q = q_ref[r0:r0 + half, cols]
            acc = jnp.zeros((half, LANES), F32)
            for hh in range(2):
                qh = q * hms[hh]
                s = [_dot(qh, kt[i]) + tab_ref[2 * pp + hh, r0:r0 + half, i * kb:(i + 1) * kb]
                     for i in blocks]
                s.append(_dot(qh, kt[4]))
                m = jnp.max(s[0], axis=-1, keepdims=True)
                for si in s[1:]:
                    m = jnp.maximum(m, jnp.max(si, axis=-1, keepdims=True))
                p = [jnp.exp2(si - m) for si in s]
                l = jnp.sum(p[0], axis=-1, keepdims=True)
                for pi in p[1:]:
                    l = l + jnp.sum(pi, axis=-1, keepdims=True)
                o = _dot(p[3].astype(BF16), vh[hh][4])
                for n_, i in enumerate(blocks):
                    o = o + _dot(p[n_].astype(BF16), vh[hh][i])
                acc = acc + o / l
            o_ref[r0:r0 + half, cols] = acc.astype(o_ref.dtype)


def _attention(q, kt, v, kct, vc, tab, b, s, l):
    n = q.shape[0]
    rows = s // GRID_W
    nb = rows // ATT_ROWS
    qblk = ATT_ROWS * GRID_W
    kpb = s // ATT_KBLK
    hg = NA_HEADS // ATT_HEADS
    hw = ATT_HEADS * NA_HEAD_DIM

    kper = ATT_ROWS * GRID_W // ATT_KBLK
    koff = (NA_KR // 2) * GRID_W // ATT_KBLK

    def kblk(j, i):
        return jnp.clip(kper * j - koff + i, 0, kpb - 1)

    qspec = pl.BlockSpec((qblk, hw), lambda h, j, bb: (bb * nb + j, h))
    ktspecs = [pl.BlockSpec((hw, ATT_KBLK), lambda h, j, bb, i=i: (h, bb * kpb + kblk(j, i)))
               for i in range(4)]
    vspecs = [pl.BlockSpec((ATT_KBLK, hw), lambda h, j, bb, i=i: (bb * kpb + kblk(j, i), h))
              for i in range(4)]
    ctspec = pl.BlockSpec((hw, l), lambda h, j, bb: (h, bb))
    cspec = pl.BlockSpec((l, hw), lambda h, j, bb: (bb, h))
    tspec = pl.BlockSpec((ATT_HEADS, None, qblk, ATT_WIN * GRID_W),
                         lambda h, j, bb: (h, jnp.where(j == 0, 0, jnp.where(j == nb - 1, 2, 1)), 0, 0))
    return pl.pallas_call(
        _attn_kernel,
        grid=(hg, nb, b),
        in_specs=[qspec] + ktspecs + [ctspec] + vspecs + [cspec, tspec],
        out_specs=qspec,
        out_shape=jax.ShapeDtypeStruct((n, NA_WIDTH), BF16),
        compiler_params=_cparams(("arbitrary", "arbitrary", "arbitrary")),
        name="nbr_attention",
    )(q, kt, kt, kt, kt, kct, v, v, v, v, vc, tab)


def _residual_norm_route(x, y, g1, n2g, sh2, sc2, rwt, rb, xo_ref, ho_ref, ri_ref, rf_ref, cnt_ref):
    tm = x.shape[0]
    x1 = x + g1 * y
    xo_ref[...] = x1
    h2 = _norm_mod(x1, n2g, sh2, sc2)
    h2_hi, h2_lo = _split_bf16(h2)
    ho_ref[...] = h2_hi
    w_hi, w_lo = _split_bf16(rwt)
    lg = _dot_nt(jnp.concatenate([w_hi, w_lo], axis=0), h2_hi)
    logits = lg[:N_EXPERTS] + lg[N_EXPERTS:] + _dot_nt(w_hi, h2_lo)
    scores = _sigmoid(logits)
    sel = scores + rb
    r = [sel[e:e + 1, :] for e in range(N_EXPERTS)]
    sc = [scores[e:e + 1, :] for e in range(N_EXPERTS)]
    grp = []
    for g in range(N_GROUPS):
        a, b_, c, d = r[4 * g:4 * g + 4]
        hi1, lo1 = jnp.maximum(a, b_), jnp.minimum(a, b_)
        hi2, lo2 = jnp.maximum(c, d), jnp.minimum(c, d)
        m1 = jnp.maximum(hi1, hi2)
        m2 = jnp.maximum(jnp.minimum(hi1, hi2), jnp.maximum(lo1, lo2))
        grp.append(m1 + m2)
    best = grp[0]
    gi = jnp.zeros(best.shape, I32)
    for g in range(1, N_GROUPS):
        upd = grp[g] > best
        best = jnp.where(upd, grp[g], best)
        gi = jnp.where(upd, g, gi)
    chosen = []
    for e in range(N_EXPERTS):
        g = e // EXPERTS_PER_GROUP
        rank = jnp.zeros(best.shape, I32)
        for e2 in range(EXPERTS_PER_GROUP * g, EXPERTS_PER_GROUP * (g + 1)):
            if e2 == e:
                continue
            ahead = (r[e2] > r[e]) | ((r[e2] == r[e]) & (e2 < e))
            rank = rank + jnp.where(ahead, 1, 0)
        chosen.append((gi == g) & (rank < 2))
    wsel = [jnp.where(chosen[e], sc[e], 0.0) for e in range(N_EXPERTS)]
    denom = wsel[0]
    for e in range(1, N_EXPERTS):
        denom = denom + wsel[e]
    selm = jnp.concatenate([jnp.where(chosen[e], 1.0, 0.0) for e in range(N_EXPERTS)], axis=0)
    ii = lax.broadcasted_iota(I32, (tm, tm), 0)
    jj = lax.broadcasted_iota(I32, (tm, tm), 1)
    tri = jnp.where(ii < jj, 1.0, 0.0).astype(BF16)
    prefix = _dot(selm.astype(BF16), tri)
    cnt = jnp.sum(selm, axis=1, keepdims=True)
    cnt_ref[...] = jnp.broadcast_to(cnt, cnt_ref.shape).astype(I32)
    seen = jnp.zeros(best.shape, I32)
    e0 = jnp.full(best.shape, -1, I32)
    e1 = jnp.full(best.shape, -1, I32)
    r0 = jnp.full(best.shape, -4 * RCOMP, I32)
    r1 = jnp.full(best.shape, -4 * RCOMP, I32)
    g0 = jnp.zeros(best.shape, F32)
    g1_ = jnp.zeros(best.shape, F32)
    for e in range(N_EXPERTS):
        first = chosen[e] & (seen == 0)
        second = chosen[e] & (seen == 1)
        pe = prefix[e:e + 1, :].astype(I32)
        ge = wsel[e] / denom
        e0 = jnp.where(first, e, e0)
        e1 = jnp.where(second, e, e1)
        r0 = jnp.where(first, pe, r0)
        r1 = jnp.where(second, pe, r1)
        g0 = jnp.where(first, ge, g0)
        g1_ = jnp.where(second, ge, g1_)
        seen = seen + jnp.where(chosen[e], 1, 0)
    ri_ref[...] = jnp.zeros(ri_ref.shape, I32)
    rf_ref[...] = jnp.zeros(rf_ref.shape, F32)
    ri_ref[0:1, :] = e0
    ri_ref[1:2, :] = e1
    ri_ref[2:3, :] = r0
    ri_ref[3:4, :] = r1
    rf_ref[0:1, :] = g0
    rf_ref[1:2, :] = g1_


def _tail_specs(n, d, tm, mod, layer, tpb):
    assert tm == TMD
    one = pl.BlockSpec((1, d), lambda t: (0, 0))
    in_specs = [pl.BlockSpec((tm, d), lambda t: (t, 0)),
                mod.vec(layer, 2, tpb), one, mod.vec(layer, 3, tpb), mod.vec(layer, 4, tpb),
                pl.BlockSpec((N_EXPERTS, d), lambda t: (0, 0)),
                pl.BlockSpec((N_EXPERTS, 1), lambda t: (0, 0))]
    out_specs = [pl.BlockSpec((tm, d), lambda t: (t, 0)),
                 pl.BlockSpec((tm, d), lambda t: (t, 0)),
                 pl.BlockSpec((SUBLANES_F32, tm), lambda t: (0, t)),
                 pl.BlockSpec((SUBLANES_F32, tm), lambda t: (0, t)),
                 pl.BlockSpec((None, N_EXPERTS, LANES), lambda t: (t, 0, 0))]
    out_shape = [jax.ShapeDtypeStruct((n, d), F32), jax.ShapeDtypeStruct((n, d), BF16),
                 jax.ShapeDtypeStruct((SUBLANES_F32, n), I32), jax.ShapeDtypeStruct((SUBLANES_F32, n), F32),
                 jax.ShapeDtypeStruct((n // tm, N_EXPERTS, LANES), I32)]
    return in_specs, out_specs, out_shape


def _outproj_kernel(tpb, a_ref, gb_ref, u_ref, up_ref, un_ref, cw_ref, wa_ref, wb_ref,
                    x_ref, g1_ref, n2g_ref, sh2_ref, sc2_ref, rwt_ref, rb_ref,
                    xo_ref, ho_ref, ri_ref, rf_ref, cnt_ref):
    t = pl.program_id(0)
    tm = u_ref.shape[0]
    u = u_ref[...].astype(F32)
    keep_prev = jnp.where(t % tpb == 0, 0.0, 1.0)
    keep_next = jnp.where(t % tpb == tpb - 1, 0.0, 1.0)
    prev_row = up_ref[HALO - 1:HALO, :].astype(F32) * keep_prev
    next_row = un_ref[0:1, :].astype(F32) * keep_next
    row = lax.broadcasted_iota(I32, u.shape, 0)
    u_m1 = jnp.where(row == 0, prev_row, pltpu.roll(u, 1, 0))
    u_p1 = jnp.where(row == tm - 1, next_row, pltpu.roll(u, tm - 1, 0))
    conv = cw_ref[0:1, :] * u_m1 + cw_ref[1:2, :] * u + cw_ref[2:3, :] * u_p1
    bx = (gb_ref[...].astype(F32) * conv).astype(BF16)
    y = _dot(a_ref[...], wa_ref[...]) + _dot(bx, wb_ref[...])
    _residual_norm_route(x_ref[...], y, g1_ref[...], n2g_ref[...], sh2_ref[...], sc2_ref[...],
                         rwt_ref[...], rb_ref[...], xo_ref, ho_ref, ri_ref, rf_ref, cnt_ref)


def _outproj(a, gb, u, cw, wa, wb, x2, mod, n2g, rwt, rb, tpb):
    n, d = x2.shape
    tm = TMD
    w = NA_WIDTH
    hb = tm // HALO
    nh = n // HALO
    half = pl.BlockSpec((tm, w), lambda t: (t, 0))
    tail_in, out_specs, out_shape = _tail_specs(n, d, tm, mod, 0, tpb)
    in_specs = [half, half, half,
                pl.BlockSpec((HALO, w), lambda t: (jnp.maximum(t * hb - 1, 0), 0)),
                pl.BlockSpec((HALO, w), lambda t: (jnp.minimum((t + 1) * hb, nh - 1), 0)),
                pl.BlockSpec(cw.shape, lambda t: (0, 0)),
                pl.BlockSpec(wa.shape, lambda t: (0, 0)),
                pl.BlockSpec(wb.shape, lambda t: (0, 0))] + tail_in
    return pl.pallas_call(
        functools.partial(_outproj_kernel, tpb),
        grid=(n // tm,),
        in_specs=in_specs, out_specs=out_specs, out_shape=out_shape,
        compiler_params=_cparams(("arbitrary",)),
        name="outproj_route",
    )(a, gb, u, u, u, cw, wa, wb, x2, mod.arr, n2g, mod.arr, mod.arr, rwt, rb)


def _conf_kernel(tpb, u_ref, up_ref, un_ref, dww_ref, dwb_ref, lng_ref, lnb_ref, w2_ref, b2_ref,
                 x_ref, g1_ref, n2g_ref, sh2_ref, sc2_ref, rwt_ref, rb_ref,
                 xo_ref, ho_ref, ri_ref, rf_ref, cnt_ref, ue_ref, sh_ref, conv_ref, wb_ref):
    t = pl.program_id(0)
    tm, d = u_ref.shape
    taps = dww_ref.shape[0]
    sl = SUBLANES_F32

    @pl.when(t == 0)
    def _():
        for k in range(taps):
            wb_ref[k * sl:(k + 1) * sl, :] = jnp.broadcast_to(dww_ref[k:k + 1, :], (sl, d))

    keep_prev = jnp.where(t % tpb == 0, 0.0, 1.0)
    keep_next = jnp.where(t % tpb == tpb - 1, 0.0, 1.0)
    ue_ref[0:HALO, :] = up_ref[...].astype(F32) * keep_prev
    ue_ref[HALO:HALO + tm, :] = u_ref[...].astype(F32)
    ue_ref[HALO + tm:HALO + tm + HALO, :] = un_ref[...].astype(F32) * keep_next
    ext = tm + 2 * HALO - sl
    for s in range(sl):
        sh_ref[s, :, :] = ue_ref[s:s + ext, :]
    off0 = HALO - (taps - 1) // 2
    nv = CONV_RC // sl
    for cb in range(d // LANES):
        cols = slice(cb * LANES, (cb + 1) * LANES)

        def body(rc, carry, cols=cols):
            r0 = pl.multiple_of(rc * CONV_RC, CONV_RC)
            accs = [None] * nv
            for k in range(taps):
                s, a = (k + off0) % sl, (k + off0) // sl
                wk = wb_ref[k * sl:(k + 1) * sl, cols]
                for j in range(nv):
                    term = wk * sh_ref[s, pl.ds(r0 + sl * (a + j), sl), cols]
                    accs[j] = term if accs[j] is None else accs[j] + term
            for j in range(nv):
                conv_ref[pl.ds(r0 + sl * j, sl), cols] = accs[j]
            return carry

        lax.fori_loop(0, tm // CONV_RC, body, 0)
    c = conv_ref[...] + dwb_ref[...]
    mu = jnp.mean(c, axis=-1, keepdims=True)
    cc = c - mu
    var = jnp.mean(cc * cc, axis=-1, keepdims=True)
    z = cc * lax.rsqrt(var + EPS) * lng_ref[...] + lnb_ref[...]
    z = (z * _sigmoid(z)).astype(BF16)
    y = _dot(z, w2_ref[...]) + b2_ref[...]
    _residual_norm_route(x_ref[...], y, g1_ref[...], n2g_ref[...], sh2_ref[...], sc2_ref[...],
                         rwt_ref[...], rb_ref[...], xo_ref, ho_ref, ri_ref, rf_ref, cnt_ref)


def _conf(u, dww, dwb, lng, lnb, w2, b2, x2, mod, n2g, rwt, rb, tpb):
    n, d = x2.shape
    tm = TMD
    hb = tm // HALO
    nh = n // HALO
    taps = dww.shape[0]
    assert (taps - 1) // 2 <= HALO - 1 and taps // 2 <= HALO
    one = pl.BlockSpec((1, d), lambda t: (0, 0))
    tail_in, out_specs, out_shape = _tail_specs(n, d, tm, mod, 1, tpb)
    in_specs = [pl.BlockSpec((tm, d), lambda t: (t, 0)),
                pl.BlockSpec((HALO, d), lambda t: (jnp.maximum(t * hb - 1, 0), 0)),
                pl.BlockSpec((HALO, d), lambda t: (jnp.minimum((t + 1) * hb, nh - 1), 0)),
                pl.BlockSpec(dww.shape, lambda t: (0, 0)), one, one, one,
                pl.BlockSpec(w2.shape, lambda t: (0, 0)), one] + tail_in
    ext = tm + 2 * HALO - SUBLANES_F32
    return pl.pallas_call(
        functools.partial(_conf_kernel, tpb),
        grid=(n // tm,),
        in_specs=in_specs, out_specs=out_specs, out_shape=out_shape,
        scratch_shapes=[pltpu.VMEM((tm + 2 * HALO, d), F32),
                        pltpu.VMEM((SUBLANES_F32, ext, d), F32),
                        pltpu.VMEM((tm, d), F32),
                        pltpu.VMEM((taps * SUBLANES_F32, d), F32)],
        compiler_params=_cparams(("arbitrary",)),
        name="conf_conv_route",
    )(u, u, u, dww, dwb, lng, lnb, w2, b2, x2, mod.arr, n2g, mod.arr, mod.arr, rwt, rb)


def _sorted_rows(n):
    t_n = n // TMD
    rows = 2 * n + (CHUNK - 1) * t_n * N_EXPERTS + N_EXPERTS * (TME - CHUNK)
    return (rows + TME - 1) // TME * TME


def _moe_plan(cnt, n):
    t_n, e_n = cnt.shape
    rt = _sorted_rows(n)
    seg = (cnt + CHUNK - 1) // CHUNK * CHUNK
    lo = jnp.cumsum(seg, axis=1) - seg
    etot = jnp.sum(seg, axis=0)
    epad = (etot + TME - 1) // TME * TME
    eend = jnp.cumsum(epad)
    estart = eend - epad
    go = estart[None, :] + jnp.cumsum(seg, axis=0) - seg
    nchunk = jnp.sum(seg, axis=1) // CHUNK
    c_rows = jnp.arange(CMAX, dtype=I32) * CHUNK
    e_ids = jnp.arange(e_n, dtype=I32)
    owner = jnp.sum(((lo + seg)[:, None, :] <= c_rows[None, :, None]).astype(I32), axis=-1)
    owner = jnp.minimum(owner, e_n - 1)
    shift = jnp.sum(jnp.where(owner[:, :, None] == e_ids, (go - lo)[:, None, :], 0), axis=-1)
    cmap = jnp.clip((shift + c_rows[None, :]) // CHUNK, 0, rt // CHUNK - 1)
    npad = (epad - etot) // CHUNK
    pend = jnp.cumsum(npad)
    kk = jnp.arange(e_n * (TME // CHUNK - 1), dtype=I32)
    pown = jnp.minimum(jnp.sum((pend[None, :] <= kk[:, None]).astype(I32), axis=-1), e_n - 1)
    pbase = (estart + etot) // CHUNK - (pend - npad)
    padmap = kk + jnp.sum(jnp.where(pown[:, None] == e_ids, pbase[None, :], 0), axis=-1)
    padmap = jnp.clip(padmap, 0, rt // CHUNK - 1)
    tile_row = jnp.arange(rt // TME, dtype=I32) * TME
    texp = jnp.minimum(jnp.sum((eend[None, :] <= tile_row[:, None]).astype(I32), axis=-1), e_n - 1)
    as_i32 = lambda a: a.astype(I32)
    return dict(lo=as_i32(lo.reshape(-1)), nchunk=as_i32(nchunk), cmap=as_i32(cmap.reshape(-1)),
                padmap=as_i32(padmap), npadtot=as_i32(pend[-1:]), texp=as_i32(texp),
                nact=as_i32(eend[-1:] // TME))


def _chunk_copy(src_ref, dst_ref, sem):
    return pltpu.make_async_copy(src_ref, dst_ref, sem)


def _wait_chunks(n_chunks, desc):
    for bit in range(CMAX.bit_length()):
        @pl.when(((n_chunks >> bit) & 1) == 1)
        def _(bit=bit):
            desc(CHUNK << bit).wait()


def _dispatch_kernel(nchunk_ref, cmap_ref, npad_ref, padmap_ref, nact_ref, lo_ref,
                     h_ref, ri_ref, xs_ref, lp_ref, xc_ref, z_ref, sem, zsem):
    t = pl.program_id(0)
    nt = pl.num_programs(0)
    slot = t % 2
    ntail = xs_ref.shape[0] // TME - nact_ref[0]

    def pad_copy(k):
        g = pl.multiple_of(padmap_ref[k] * CHUNK, CHUNK)
        return _chunk_copy(z_ref.at[0:CHUNK, :], xs_ref.at[pl.ds(g, CHUNK), :], zsem.at[0])

    def tail_copy(k):
        g = pl.multiple_of((nact_ref[0] + k) * TME, TME)
        return _chunk_copy(z_ref, xs_ref.at[pl.ds(g, TME), :], zsem.at[0])

    @pl.when(t == 0)
    def _():
        z_ref[...] = jnp.zeros(z_ref.shape, z_ref.dtype)
        lax.fori_loop(0, npad_ref[0], lambda k, c: (pad_copy(k).start(), c)[1], 0)
        lax.fori_loop(0, ntail, lambda k, c: (tail_copy(k).start(), c)[1], 0)

    e0, e1 = ri_ref[0:1, :], ri_ref[1:2, :]
    b0 = jnp.zeros(e0.shape, I32)
    b1 = jnp.zeros(e0.shape, I32)
    for e in range(N_EXPERTS):
        lo_e = lo_ref[t * N_EXPERTS + e]
        b0 = jnp.where(e0 == e, lo_e, b0)
        b1 = jnp.where(e1 == e, lo_e, b1)
    lp0 = b0 + ri_ref[2:3, :]
    lp1 = b1 + ri_ref[3:4, :]
    lp_ref[...] = jnp.zeros(lp_ref.shape, I32)
    lp_ref[0:1, :] = lp0
    lp_ref[1:2, :] = lp1

    rows = lax.broadcasted_iota(I32, (RCOMP, TMD), 0)
    onehot = jnp.where((lp0 == rows) | (lp1 == rows), 1.0, 0.0).astype(BF16)
    xc_ref[slot] = _dot(onehot, h_ref[...]).astype(BF16)

    def out_copy(tt, sl, c):
        g = pl.multiple_of(cmap_ref[tt * CMAX + c] * CHUNK, CHUNK)
        l = pl.multiple_of(c * CHUNK, CHUNK)
        return _chunk_copy(xc_ref.at[sl, pl.ds(l, CHUNK), :], xs_ref.at[pl.ds(g, CHUNK), :], sem.at[sl])

    lax.fori_loop(0, nchunk_ref[t], lambda c, cr: (out_copy(t, slot, c).start(), cr)[1], 0)

    def out_desc(sl):
        return lambda rows: _chunk_copy(xc_ref.at[sl, 0:rows, :], xs_ref.at[0:rows, :], sem.at[sl])

    @pl.when(t > 0)
    def _():
        _wait_chunks(nchunk_ref[t - 1], out_desc(1 - slot))

    @pl.when(t == 0)
    def _():
        lax.fori_loop(0, npad_ref[0], lambda k, c: (pad_copy(k).wait(), c)[1], 0)
        lax.fori_loop(0, ntail, lambda k, c: (tail_copy(k).wait(), c)[1], 0)

    @pl.when(t == nt - 1)
    def _():
        _wait_chunks(nchunk_ref[t], out_desc(slot))


def _dispatch(h2, ri, plan):
    n, d = h2.shape
    t_n = n // TMD
    rt = _sorted_rows(n)
    grid_spec = pltpu.PrefetchScalarGridSpec(
        num_scalar_prefetch=6,
        grid=(t_n,),
        in_specs=[pl.BlockSpec((TMD, d), lambda t, *_: (t, 0)),
                  pl.BlockSpec((SUBLANES_F32, TMD), lambda t, *_: (0, t))],
        out_specs=[pl.BlockSpec(memory_space=pl.ANY),
                   pl.BlockSpec((None, SUBLANES_F32, TMD), lambda t, *_: (t, 0, 0))],
        scratch_shapes=[pltpu.VMEM((2, RCOMP, d), BF16),
                        pltpu.VMEM((TME, d), BF16),
                        pltpu.SemaphoreType.DMA((2,)),
                        pltpu.SemaphoreType.DMA((1,))],
    )
    return pl.pallas_call(
        _dispatch_kernel,
        grid_spec=grid_spec,
        out_shape=[jax.ShapeDtypeStruct((rt, d), BF16),
                   jax.ShapeDtypeStruct((t_n, SUBLANES_F32, TMD), I32)],
        compiler_params=_cparams(("arbitrary",)),
        name="moe_dispatch",
    )(plan["nchunk"], plan["cmap"], plan["npadtot"], plan["padmap"], plan["nact"], plan["lo"], h2, ri)


def _expert_kernel(texp_ref, nact_ref, x_ref, w1_ref, w3_ref, w2_ref, y_ref, w1b, w3b, w2b):
    i = pl.program_id(0)

    @pl.when(i < nact_ref[0])
    def _():
        prev = texp_ref[jnp.maximum(i - 1, 0)]

        @pl.when((i == 0) | (texp_ref[i] != prev))
        def _():
            w1b[...] = w1_ref[...].astype(BF16)
            w3b[...] = w3_ref[...].astype(BF16)
            w2b[...] = w2_ref[...].astype(BF16)

        x = x_ref[...]
        h = _dot(x, w1b[...])
        g = _dot(x, w3b[...])
        a = (h * _sigmoid(h) * g).astype(BF16)
        y_ref[...] = _dot(a, w2b[...]).astype(y_ref.dtype)

    @pl.when(i >= nact_ref[0])
    def _():
        y_ref[...] = jnp.zeros(y_ref.shape, y_ref.dtype)


def _experts(xs, w1, w3, w2, layer, plan):
    rt, d = xs.shape
    f = w1.shape[-1]
    nt = rt // TME

    def row_map(i, texp, nact):
        return (jnp.clip(i, 0, jnp.maximum(nact[0] - 1, 0)), 0)

    def w_map(i, texp, nact):
        return (layer, texp[jnp.clip(i, 0, jnp.maximum(nact[0] - 1, 0))], 0, 0)

    grid_spec = pltpu.PrefetchScalarGridSpec(
        num_scalar_prefetch=2,
        grid=(nt,),
        in_specs=[pl.BlockSpec((TME, d), row_map),
                  pl.BlockSpec((None, None, d, f), w_map),
                  pl.BlockSpec((None, None, d, f), w_map),
                  pl.BlockSpec((None, None, f, d), w_map)],
        out_specs=pl.BlockSpec((TME, d), lambda i, texp, nact: (i, 0)),
        scratch_shapes=[pltpu.VMEM((d, f), BF16), pltpu.VMEM((d, f), BF16), pltpu.VMEM((f, d), BF16)],
    )
    return pl.pallas_call(
        _expert_kernel,
        grid_spec=grid_spec,
        out_shape=jax.ShapeDtypeStruct((rt, d), BF16),
        compiler_params=_cparams(("arbitrary",)),
        name="moe_experts",
    )(plan["texp"], plan["nact"], xs, w1, w3, w2)


def _combine_core(nchunk_ref, cmap_ref, ys_ref, lp0_ref, lp1_ref, g0_ref, g1_ref, x_ref, g2_ref,
                  yc_ref, sem):
    t = pl.program_id(0)
    nt = pl.num_programs(0)
    slot = t % 2

    def in_copy(tt, sl, c):
        g = pl.multiple_of(cmap_ref[tt * CMAX + c] * CHUNK, CHUNK)
        l = pl.multiple_of(c * CHUNK, CHUNK)
        return _chunk_copy(ys_ref.at[pl.ds(g, CHUNK), :], yc_ref.at[sl, pl.ds(l, CHUNK), :], sem.at[sl])

    @pl.when(t == 0)
    def _():
        yc_ref[...] = jnp.zeros(yc_ref.shape, yc_ref.dtype)
        lax.fori_loop(0, nchunk_ref[0], lambda c, cr: (in_copy(0, 0, c).start(), cr)[1], 0)

    @pl.when(t + 1 < nt)
    def _():
        lax.fori_loop(0, nchunk_ref[t + 1], lambda c, cr: (in_copy(t + 1, 1 - slot, c).start(), cr)[1], 0)

    _wait_chunks(nchunk_ref[t],
                 lambda rows: _chunk_copy(ys_ref.at[0:rows, :], yc_ref.at[slot, 0:rows, :], sem.at[slot]))

    cols = lax.broadcasted_iota(I32, (TMD, RCOMP), 1)
    w = jnp.where(lp0_ref[...] == cols, g0_ref[...], 0.0) + jnp.where(lp1_ref[...] == cols, g1_ref[...], 0.0)
    moe = _dot(w.astype(BF16), yc_ref[slot])
    return x_ref[...] + g2_ref[...] * moe


def _combine_final_kernel(nchunk_ref, cmap_ref, ys_ref, lp0_ref, lp1_ref, g0_ref, g1_ref,
                          x_ref, g2_ref, fg_ref, o_ref, yc_ref, sem):
    x1 = _combine_core(nchunk_ref, cmap_ref, ys_ref, lp0_ref, lp1_ref, g0_ref, g1_ref, x_ref, g2_ref,
                       yc_ref, sem)
    ms = jnp.mean(x1 * x1, axis=-1, keepdims=True)
    o_ref[...] = x1 * lax.rsqrt(ms + EPS) * fg_ref[...]


def _combine_pw1_kernel(nchunk_ref, cmap_ref, ys_ref, lp0_ref, lp1_ref, g0_ref, g1_ref,
                        x_ref, g2_ref, n1g_ref, sh_ref, sc_ref, w_ref, b_ref, o_ref, u_ref, yc_ref, sem):
    x1 = _combine_core(nchunk_ref, cmap_ref, ys_ref, lp0_ref, lp1_ref, g0_ref, g1_ref, x_ref, g2_ref,
                       yc_ref, sem)
    o_ref[...] = x1
    d = x1.shape[1]
    h = _norm_mod(x1, n1g_ref[...], sh_ref[...], sc_ref[...]).astype(BF16)
    a = _dot(h, w_ref[:, 0:d]) + b_ref[:, 0:d]
    g = _dot(h, w_ref[:, d:2 * d]) + b_ref[:, d:2 * d]
    u_ref[...] = (a * _sigmoid(g)).astype(BF16)


def _combine(ys, plan, lp, rf, x2, mod, layer, tpb, tail):
    n, d = x2.shape
    t_n = n // TMD
    col = pl.BlockSpec((TMD, 1), lambda t, *_: (t, 0))
    one = pl.BlockSpec((1, d), lambda t, *_: (0, 0))
    tile = pl.BlockSpec((TMD, d), lambda t, *_: (t, 0))
    in_specs = [pl.BlockSpec(memory_space=pl.ANY), col, col, col, col, tile, mod.vec(layer, 5, tpb)]
    args = [plan["nchunk"], plan["cmap"], ys,
            lp[:, 0, :].reshape(n, 1), lp[:, 1, :].reshape(n, 1), rf[0].reshape(n, 1), rf[1].reshape(n, 1),
            x2, mod.arr]
    if tail[0] == "final":
        body, name = _combine_final_kernel, "moe_combine_final"
        in_specs += [one]
        args += [tail[1]]
        out_specs = tile
        out_shape = jax.ShapeDtypeStruct((n, d), F32)
    else:
        body, name = _combine_pw1_kernel, "moe_combine_pw1"
        _, n1g, w_bf, bias = tail
        in_specs += [one, mod.vec(layer + 1, 0, tpb), mod.vec(layer + 1, 1, tpb),
                     pl.BlockSpec(w_bf.shape, lambda t, *_: (0, 0)),
                     pl.BlockSpec(bias.shape, lambda t, *_: (0, 0))]
        args += [n1g, mod.arr, mod.arr, w_bf, bias]
        out_specs = [tile, tile]
        out_shape = [jax.ShapeDtypeStruct((n, d), F32), jax.ShapeDtypeStruct((n, d), BF16)]
    grid_spec = pltpu.PrefetchScalarGridSpec(
        num_scalar_prefetch=2,
        grid=(t_n,),
        in_specs=in_specs,
        out_specs=out_specs,
        scratch_shapes=[pltpu.VMEM((2, RCOMP, d), BF16), pltpu.SemaphoreType.DMA((2,))],
    )
    return pl.pallas_call(body, grid_spec=grid_spec, out_shape=out_shape,
                          compiler_params=_cparams(("arbitrary",)), name=name)(*args)


def _moe(x1, h2, ri, rf, cnt, w1, w3, w2, layer, mod, tpb, tail):
    n = x1.shape[0]
    plan = _moe_plan(cnt[:, :, 0], n)
    xs, lp = _dispatch(h2, ri, plan)
    ys = _experts(xs, w1, w3, w2, layer, plan)
    return _combine(ys, plan, lp, rf, x1, mod, layer, tpb, tail)


def kernel(x, c, ctx, c_ctx, norm1_g, norm2_g, w_mod, b_mod, na_w_in, na_rpb, sc_conv_w, ab_w_out,
           cf_pw1_w, cf_pw1_b, cf_dw_w, cf_dw_b, cf_ln_g, cf_ln_b, cf_pw2_w, cf_pw2_b,
           router_w, router_b, moe_w1, moe_w3, moe_w2, final_g):
    b, s, d = x.shape
    l = ctx.shape[1]
    n = b * s
    depth = w_mod.shape[0]
    assert depth == 2, "layer pattern (attention layer, Conformer layer) is written out for depth 2"
    assert s % TM_PROJ == 0 and s % TMD == 0
    rows = s // GRID_W
    assert rows % ATT_ROWS == 0 and rows >= ATT_WIN and s % ATT_KBLK == 0
    tpb = s // TM_PROJ
    tpb_d = s // TMD

    mr = (b + 1 + SUBLANES_F32 - 1) // SUBLANES_F32 * SUBLANES_F32
    c_all = jnp.concatenate([c, c_ctx[None, :], jnp.zeros((mr - b - 1, d), F32)], axis=0)
    mod = _Mod(_modulation(c_all, w_mod, b_mod), d)

    x2 = x.reshape(n, d)
    rwt = router_w.T
    rb = router_b.reshape(N_EXPERTS, 1)
    fg = final_g.reshape(1, d)

    w_in = na_w_in[0].astype(BF16)
    n1g = norm1_g[0].reshape(1, d)
    wkt = w_in[:, NA_WIDTH:2 * NA_WIDTH].T
    q, kt, v, gb, u = _inproj(x2, n1g, mod, w_in, wkt, tpb)
    kct, vc = _ctxproj(ctx.reshape(b * l, d), n1g, mod, b, wkt, w_in[:, 2 * NA_WIDTH:3 * NA_WIDTH], l)
    tab = _attn_bias_tables(na_rpb[0], rows)
    a = _attention(q, kt, v, kct, vc, tab, b, s, l)
    w_out = ab_w_out[0].astype(BF16)
    x1, h2, ri, rf, cnt = _outproj(a, gb, u, sc_conv_w[0], w_out[:NA_WIDTH], w_out[NA_WIDTH:], x2,
                                   mod, norm2_g[0].reshape(1, d), rwt, rb, tpb_d)
    x2, u = _moe(x1, h2, ri, rf, cnt, moe_w1, moe_w3, moe_w2, 0, mod, tpb_d,
                 ("pw1", norm1_g[1].reshape(1, d), cf_pw1_w[0].astype(BF16), cf_pw1_b[0].reshape(1, 2 * d)))
    x1, h2, ri, rf, cnt = _conf(u, cf_dw_w[0], cf_dw_b[0].reshape(1, d), cf_ln_g[0].reshape(1, d),
                                cf_ln_b[0].reshape(1, d), cf_pw2_w[0].astype(BF16),
                                cf_pw2_b[0].reshape(1, d), x2, mod, norm2_g[1].reshape(1, d),
                                rwt, rb, tpb_d)
    out = _moe(x1, h2, ri, rf, cnt, moe_w1, moe_w3, moe_w2, 1, mod, tpb_d, ("final", fg))
    return out.reshape(b, s, d)
```

```python
import functools

import numpy as np
import jax
import jax.numpy as jnp
from jax import lax
from jax.experimental import pallas as pl
from jax.experimental.pallas import tpu as pltpu

F32 = jnp.float32
BF16 = jnp.bfloat16
I32 = jnp.int32

GRID_W = 64
NA_HEADS = 8
NA_HEAD_DIM = 64
NA_WIDTH = NA_HEADS * NA_HEAD_DIM
NA_KR = 8
NA_KC = 16
N_EXPERTS = 16
N_GROUPS = 4
EXPERTS_PER_GROUP = N_EXPERTS // N_GROUPS
N_MOD = 6
EPS = 1e-6
NEG = -1e30
LOG2E = 1.4426950408889634

LANES = 128
SUBLANES_F32 = 8
SUBLANES_BF16 = 16

TM_PROJ = 512
ATT_ROWS = 8
ATT_WIN = 2 * ATT_ROWS
ATT_KBLK = 256
ATT_HEADS = 4
TMD = 512
CHUNK = SUBLANES_BF16
CMAX = (2 * TMD + (N_EXPERTS - 1) * CHUNK + CHUNK - 1) // CHUNK + 1
RCOMP = CMAX * CHUNK
TME = 1024
CONV_RC = 128
HALO = SUBLANES_BF16
VMEM_LIMIT = 52 * 1024 * 1024


def _cparams(sem):
    return pltpu.CompilerParams(dimension_semantics=sem, vmem_limit_bytes=VMEM_LIMIT)


def _dot(a, b):
    return jnp.dot(a, b, preferred_element_type=F32)


def _dot_nt(a, b, precision=None):
    return lax.dot_general(a, b, (((1,), (1,)), ((), ())), precision=precision,
                           preferred_element_type=F32)


def _sigmoid(x):
    return 1.0 / (1.0 + jnp.exp(-x))


def _norm_mod(x, g, sh, sc):
    ms = jnp.mean(x * x, axis=-1, keepdims=True)
    y = x * lax.rsqrt(ms + EPS) * g
    return y * (1.0 + sc) + sh


def _split_bf16(a):
    hi = a.astype(BF16)
    lo = (a - hi.astype(F32)).astype(BF16)
    return hi, lo


def _mod_kernel(c_ref, w_ref, b_ref, o_ref):
    c = c_ref[...]
    mr = c.shape[0]
    hi, lo = _split_bf16(c * _sigmoid(c))
    w_hi, w_lo = _split_bf16(w_ref[...])
    r = _dot(jnp.concatenate([hi, lo], axis=0), w_hi)
    o_ref[...] = r[:mr] + r[mr:] + _dot(hi, w_lo) + b_ref[...]


def _modulation(c_all, w_mod, b_mod):
    depth, d, n6 = w_mod.shape
    mr = c_all.shape[0]
    tn = 1024
    return pl.pallas_call(
        _mod_kernel,
        grid=(depth, n6 // tn),
        in_specs=[pl.BlockSpec((mr, d), lambda i, j: (0, 0)),
                  pl.BlockSpec((None, d, tn), lambda i, j: (i, 0, j)),
                  pl.BlockSpec((None, 1, tn), lambda i, j: (i, 0, j))],
        out_specs=pl.BlockSpec((None, mr, tn), lambda i, j: (i, 0, j)),
        out_shape=jax.ShapeDtypeStruct((depth, mr, n6), F32),
        compiler_params=_cparams(("arbitrary", "arbitrary")),
        name="modulation",
    )(c_all, w_mod, b_mod.reshape(depth, 1, n6))


class _Mod:
    def __init__(self, mod, d):
        self.depth, self.mr, _ = mod.shape
        self.d = d
        self.arr = mod.reshape(self.depth * self.mr * N_MOD, 1, d)

    def vec(self, layer, j, tpb):
        mr = self.mr
        return pl.BlockSpec((None, 1, self.d),
                            lambda t, *_: ((layer * mr + t // tpb) * N_MOD + j, 0, 0))

    def row(self, layer, j, r):
        mr = self.mr
        return pl.BlockSpec((None, 1, self.d), lambda t, *_: ((layer * mr + r) * N_MOD + j, 0, 0))


def _inproj_kernel(x_ref, g_ref, sh_ref, sc_ref, w_ref, wkt_ref, q_ref, kt_ref, v_ref, gb_ref, u_ref):
    h = _norm_mod(x_ref[...], g_ref[...], sh_ref[...], sc_ref[...]).astype(BF16)
    w = NA_WIDTH
    q_ref[...] = (_dot(h, w_ref[:, 0:w]) * (NA_HEAD_DIM ** -0.5 * LOG2E)).astype(BF16)
    kt_ref[...] = _dot_nt(wkt_ref[...], h).astype(BF16)
    v_ref[...] = _dot(h, w_ref[:, 2 * w:3 * w]).astype(BF16)
    gb_ref[...] = _dot(h, w_ref[:, 3 * w:4 * w]).astype(BF16)
    u_ref[...] = (_dot(h, w_ref[:, 4 * w:5 * w]) * _dot(h, w_ref[:, 5 * w:6 * w])).astype(BF16)


def _inproj(x2, g, mod, w_bf, wkt_bf, tpb):
    n, d = x2.shape
    tm = TM_PROJ
    out = jax.ShapeDtypeStruct((n, NA_WIDTH), BF16)
    ospec = pl.BlockSpec((tm, NA_WIDTH), lambda t: (t, 0))
    out_t = jax.ShapeDtypeStruct((NA_WIDTH, n), BF16)
    ospec_t = pl.BlockSpec((NA_WIDTH, tm), lambda t: (0, t))
    return pl.pallas_call(
        _inproj_kernel,
        grid=(n // tm,),
        in_specs=[pl.BlockSpec((tm, d), lambda t: (t, 0)),
                  pl.BlockSpec((1, d), lambda t: (0, 0)), mod.vec(0, 0, tpb), mod.vec(0, 1, tpb),
                  pl.BlockSpec(w_bf.shape, lambda t: (0, 0)),
                  pl.BlockSpec(wkt_bf.shape, lambda t: (0, 0))],
        out_specs=[ospec, ospec_t, ospec, ospec, ospec],
        out_shape=[out, out_t, out, out, out],
        compiler_params=_cparams(("arbitrary",)),
        name="inproj",
    )(x2, g, mod.arr, mod.arr, w_bf, wkt_bf)


def _ctxproj_kernel(x_ref, g_ref, sh_ref, sc_ref, wkt_ref, wv_ref, kt_ref, v_ref):
    h = _norm_mod(x_ref[...], g_ref[...], sh_ref[...], sc_ref[...]).astype(BF16)
    kt_ref[...] = _dot_nt(wkt_ref[...], h).astype(BF16)
    v_ref[...] = _dot(h, wv_ref[...]).astype(BF16)


def _ctxproj(ctx2, g, mod, ctx_row, wkt_bf, wv_bf, l):
    n, d = ctx2.shape
    return pl.pallas_call(
        _ctxproj_kernel,
        grid=(n // l,),
        in_specs=[pl.BlockSpec((l, d), lambda t: (t, 0)), pl.BlockSpec((1, d), lambda t: (0, 0)),
                  mod.row(0, 0, ctx_row), mod.row(0, 1, ctx_row),
                  pl.BlockSpec(wkt_bf.shape, lambda t: (0, 0)),
                  pl.BlockSpec(wv_bf.shape, lambda t: (0, 0))],
        out_specs=[pl.BlockSpec((NA_WIDTH, l), lambda t: (0, t)),
                   pl.BlockSpec((l, NA_WIDTH), lambda t: (t, 0))],
        out_shape=[jax.ShapeDtypeStruct((NA_WIDTH, n), BF16), jax.ShapeDtypeStruct((n, NA_WIDTH), BF16)],
        compiler_params=_cparams(("arbitrary",)),
        name="ctxproj",
    )(ctx2, g, mod.arr, mod.arr, wkt_bf, wv_bf)


def _bias_row_index(rows):
    nb = rows // ATT_ROWS
    n_dr = 2 * NA_KR - 1
    dr_idx = np.full((3, ATT_ROWS, ATT_WIN), n_dr, np.int32)
    for ty, j in enumerate((0, 1, nb - 1)):
        ws = ATT_ROWS * j - NA_KR // 2
        for qr in range(ATT_ROWS):
            r = ATT_ROWS * j + qr
            rs = int(np.clip(r - NA_KR // 2, 0, rows - NA_KR))
            for kr in range(ATT_WIN):
                key_row = ws + kr
                if rs <= key_row < rs + NA_KR:
                    dr_idx[ty, qr, kr] = key_row - r + (NA_KR - 1)
    kr_blk = ATT_KBLK // GRID_W
    assert (dr_idx[:, :ATT_ROWS // 2, ATT_WIN - kr_blk:] == n_dr).all()
    assert (dr_idx[:, ATT_ROWS // 2:, :kr_blk] == n_dr).all()
    return dr_idx


def _bias_kernel(dr_idx, rpb_ref, out_ref, t_ref):
    h = pl.program_id(0)
    n_dr = 2 * NA_KR - 1
    n_dc = 2 * NA_KC - 1
    qc = lax.broadcasted_iota(I32, (GRID_W, GRID_W), 0)
    kc = lax.broadcasted_iota(I32, (GRID_W, GRID_W), 1)
    cs = jnp.clip(qc - NA_KC // 2, 0, GRID_W - NA_KC)
    valid = (kc >= cs) & (kc < cs + NA_KC)
    dc = jnp.where(valid, kc - qc + (NA_KC - 1), -1)
    for dr in range(n_dr):
        slab = jnp.full((GRID_W, GRID_W), NEG, F32)
        for m in range(n_dc):
            slab = jnp.where(dc == m, rpb_ref[(h * n_dr + dr) * n_dc + m] * LOG2E, slab)
        t_ref[dr] = slab
    t_ref[n_dr] = jnp.full((GRID_W, GRID_W), NEG, F32)
    for ty in range(3):
        for qr in range(ATT_ROWS):
            for kr in range(ATT_WIN):
                out_ref[ty, qr * GRID_W:(qr + 1) * GRID_W, kr * GRID_W:(kr + 1) * GRID_W] = \
                    t_ref[int(dr_idx[ty, qr, kr])]


def _attn_bias_tables(rpb, rows):
    h = rpb.shape[0]
    dr_idx = _bias_row_index(rows)
    qn, kn = ATT_ROWS * GRID_W, ATT_WIN * GRID_W
    return pl.pallas_call(
        functools.partial(_bias_kernel, dr_idx),
        grid=(h,),
        in_specs=[pl.BlockSpec(memory_space=pltpu.SMEM)],
        out_specs=pl.BlockSpec((None, 3, qn, kn), lambda i: (i, 0, 0, 0)),
        out_shape=jax.ShapeDtypeStruct((h, 3, qn, kn), F32),
        scratch_shapes=[pltpu.VMEM((2 * NA_KR, GRID_W, GRID_W), F32)],
        compiler_params=_cparams(("arbitrary",)),
        name="attn_bias_table",
    )(rpb.astype(F32).reshape(-1))


def _attn_kernel(q_ref, k0_ref, k1_ref, k2_ref, k3_ref, kc_ref,
                 v0_ref, v1_ref, v2_ref, v3_ref, vc_ref, tab_ref, o_ref):
    lane = lax.broadcasted_iota(I32, (1, LANES), 1)
    kt_refs = [k0_ref, k1_ref, k2_ref, k3_ref, kc_ref]
    v_refs = [v0_ref, v1_ref, v2_ref, v3_ref, vc_ref]
    kb = ATT_KBLK
    half = q_ref.shape[0] // 2
    hms = [jnp.where((lane >= hh * NA_HEAD_DIM) & (lane < (hh + 1) * NA_HEAD_DIM), 1.0, 0.0).astype(BF16)
           for hh in range(2)]
    hmf = [hm.astype(F32) for hm in hms]
    for pp in range(ATT_HEADS // 2):
        cols = slice(pp * LANES, (pp + 1) * LANES)
        kt = [r[cols, :] for r in kt_refs]
        vv = [r[:, cols] for r in v_refs]
        for r0, blocks in ((0, (0, 1, 2)), (half, (1, 2, 3))):
            q = q_ref[r0:r0 + half, cols]
            qs = jnp.concatenate([q * hms[0], q * hms[1]], axis=0)
            s = [_dot(qs, kt[i])
                 + jnp.concatenate([tab_ref[2 * pp, r0:r0 + half, i * kb:(i + 1) * kb],
                                    tab_ref[2 * pp + 1, r0:r0 + half, i * kb:(i + 1) * kb]], axis=0)
                 for i in blocks]
            s.append(_dot(qs, kt[4]))
            m = jnp.max(s[0], axis=-1, keepdims=True)
            for si in s[1:]:
                m = jnp.maximum(m, jnp.max(si, axis=-1, keepdims=True))
            p = [jnp.exp2(si - m) for si in s]
            l = jnp.sum(p[0], axis=-1, keepdims=True)
            for pi in p[1:]:
                l = l + jnp.sum(pi, axis=-1, keepdims=True)
            o = _dot(p[3].astype(BF16), vv[4])
            for n_, i in enumerate(blocks):
                o = o + _dot(p[n_].astype(BF16), vv[i])
            o = o / l
            acc = o[:half] * hmf[0] + o[half:] * hmf[1]
            o_ref[r0:r0 + half, cols] = acc.astype(o_ref.dtype)


def _attention(q, kt, v, kct, vc, tab, b, s, l):
    n = q.shape[0]
    rows = s // GRID_W
    nb = rows // ATT_ROWS
    qblk = ATT_ROWS * GRID_W
    kpb = s // ATT_KBLK
    hg = NA_HEADS // ATT_HEADS
    hw = ATT_HEADS * NA_HEAD_DIM

    kper = ATT_ROWS * GRID_W // ATT_KBLK
    koff = (NA_KR // 2) * GRID_W // ATT_KBLK

    def kblk(j, i):
        return jnp.clip(kper * j - koff + i, 0, kpb - 1)

    qspec = pl.BlockSpec((qblk, hw), lambda h, j, bb: (bb * nb + j, h))
    ktspecs = [pl.BlockSpec((hw, ATT_KBLK), lambda h, j, bb, i=i: (h, bb * kpb + kblk(j, i)))
               for i in range(4)]
    vspecs = [pl.BlockSpec((ATT_KBLK, hw), lambda h, j, bb, i=i: (bb * kpb + kblk(j, i), h))
              for i in range(4)]
    ctspec = pl.BlockSpec((hw, l), lambda h, j, bb: (h, bb))
    cspec = pl.BlockSpec((l, hw), lambda h, j, bb: (bb, h))
    tspec = pl.BlockSpec((ATT_HEADS, None, qblk, ATT_WIN * GRID_W),
                         lambda h, j, bb: (h, jnp.where(j == 0, 0, jnp.where(j == nb - 1, 2, 1)), 0, 0))
    return pl.pallas_call(
        _attn_kernel,
        grid=(hg, nb, b),
        in_specs=[qspec] + ktspecs + [ctspec] + vspecs + [cspec, tspec],
        out_specs=qspec,
        out_shape=jax.ShapeDtypeStruct((n, NA_WIDTH), BF16),
        compiler_params=_cparams(("arbitrary", "arbitrary", "arbitrary")),
        name="nbr_attention",
    )(q, kt, kt, kt, kt, kct, v, v, v, v, vc, tab)


def _residual_norm_route(x, y, g1, n2g, sh2, sc2, rwt, rb, xo_ref, ho_ref, ri_ref, rf_ref, cnt_ref):
    tm = x.shape[0]
    x1 = x + g1 * y
    xo_ref[...] = x1
    h2 = _norm_mod(x1, n2g, sh2, sc2)
    h2_hi, h2_lo = _split_bf16(h2)
    ho_ref[...] = h2_hi
    w_hi, w_lo = _split_bf16(rwt)
    lg = _dot_nt(jnp.concatenate([w_hi, w_lo], axis=0), h2_hi)
    logits = lg[:N_EXPERTS] + lg[N_EXPERTS:] + _dot_nt(w_hi, h2_lo)
    scores = _sigmoid(logits)
    sel = scores + rb
    r = [sel[e:e + 1, :] for e in range(N_EXPERTS)]
    sc = [scores[e:e + 1, :] for e in range(N_EXPERTS)]
    grp = []
    for g in range(N_GROUPS):
        a, b_, c, d = r[4 * g:4 * g + 4]
        hi1, lo1 = jnp.maximum(a, b_), jnp.minimum(a, b_)
        hi2, lo2 = jnp.maximum(c, d), jnp.minimum(c, d)
        m1 = jnp.maximum(hi1, hi2)
        m2 = jnp.maximum(jnp.minimum(hi1, hi2), jnp.maximum(lo1, lo2))
        grp.append(m1 + m2)
    best = grp[0]
    gi = jnp.zeros(best.shape, I32)
    for g in range(1, N_GROUPS):
        upd = grp[g] > best
        best = jnp.where(upd, grp[g], best)
        gi = jnp.where(upd, g, gi)
    chosen = []
    for e in range(N_EXPERTS):
        g = e // EXPERTS_PER_GROUP
        rank = jnp.zeros(best.shape, I32)
        for e2 in range(EXPERTS_PER_GROUP * g, EXPERTS_PER_GROUP * (g + 1)):
            if e2 == e:
                continue
            ahead = (r[e2] > r[e]) | ((r[e2] == r[e]) & (e2 < e))
            rank = rank + jnp.where(ahead, 1, 0)
        chosen.append((gi == g) & (rank < 2))
    wsel = [jnp.where(chosen[e], sc[e], 0.0) for e in range(N_EXPERTS)]
    denom = wsel[0]
    for e in range(1, N_EXPERTS):
        denom = denom + wsel[e]
    selm = jnp.concatenate([jnp.where(chosen[e], 1.0, 0.0) for e in range(N_EXPERTS)], axis=0)
    ii = lax.broadcasted_iota(I32, (tm, tm), 0)
    jj = lax.broadcasted_iota(I32, (tm, tm), 1)
    tri = jnp.where(ii < jj, 1.0, 0.0).astype(BF16)
    prefix = _dot(selm.astype(BF16), tri)
    cnt = jnp.sum(selm, axis=1, keepdims=True)
    cnt_ref[...] = jnp.broadcast_to(cnt, cnt_ref.shape).astype(I32)
    seen = jnp.zeros(best.shape, I32)
    e0 = jnp.full(best.shape, -1, I32)
    e1 = jnp.full(best.shape, -1, I32)
    r0 = jnp.full(best.shape, -4 * RCOMP, I32)
    r1 = jnp.full(best.shape, -4 * RCOMP, I32)
    g0 = jnp.zeros(best.shape, F32)
    g1_ = jnp.zeros(best.shape, F32)
    for e in range(N_EXPERTS):
        first = chosen[e] & (seen == 0)
        second = chosen[e] & (seen == 1)
        pe = prefix[e:e + 1, :].astype(I32)
        ge = wsel[e] / denom
        e0 = jnp.where(first, e, e0)
        e1 = jnp.where(second, e, e1)
        r0 = jnp.where(first, pe, r0)
        r1 = jnp.where(second, pe, r1)
        g0 = jnp.where(first, ge, g0)
        g1_ = jnp.where(second, ge, g1_)
        seen = seen + jnp.where(chosen[e], 1, 0)
    ri_ref[...] = jnp.zeros(ri_ref.shape, I32)
    rf_ref[...] = jnp.zeros(rf_ref.shape, F32)
    ri_ref[0:1, :] = e0
    ri_ref[1:2, :] = e1
    ri_ref[2:3, :] = r0
    ri_ref[3:4, :] = r1
    rf_ref[0:1, :] = g0
    rf_ref[1:2, :] = g1_


def _tail_specs(n, d, tm, mod, layer, tpb):
    assert tm == TMD
    one = pl.BlockSpec((1, d), lambda t: (0, 0))
    in_specs = [pl.BlockSpec((tm, d), lambda t: (t, 0)),
                mod.vec(layer, 2, tpb), one, mod.vec(layer, 3, tpb), mod.vec(layer, 4, tpb),
                pl.BlockSpec((N_EXPERTS, d), lambda t: (0, 0)),
                pl.BlockSpec((N_EXPERTS, 1), lambda t: (0, 0))]
    out_specs = [pl.BlockSpec((tm, d), lambda t: (t, 0)),
                 pl.BlockSpec((tm, d), lambda t: (t, 0)),
                 pl.BlockSpec((SUBLANES_F32, tm), lambda t: (0, t)),
                 pl.BlockSpec((SUBLANES_F32, tm), lambda t: (0, t)),
                 pl.BlockSpec((None, N_EXPERTS, LANES), lambda t: (t, 0, 0))]
    out_shape = [jax.ShapeDtypeStruct((n, d), F32), jax.ShapeDtypeStruct((n, d), BF16),
                 jax.ShapeDtypeStruct((SUBLANES_F32, n), I32), jax.ShapeDtypeStruct((SUBLANES_F32, n), F32),
                 jax.ShapeDtypeStruct((n // tm, N_EXPERTS, LANES), I32)]
    return in_specs, out_specs, out_shape


def _outproj_kernel(tpb, a_ref, gb_ref, u_ref, up_ref, un_ref, cw_ref, wa_ref, wb_ref,
                    x_ref, g1_ref, n2g_ref, sh2_ref, sc2_ref, rwt_ref, rb_ref,
                    xo_ref, ho_ref, ri_ref, rf_ref, cnt_ref):
    t = pl.program_id(0)
    tm = u_ref.shape[0]
    u = u_ref[...].astype(F32)
    keep_prev = jnp.where(t % tpb == 0, 0.0, 1.0)
    keep_next = jnp.where(t % tpb == tpb - 1, 0.0, 1.0)
    prev_row = up_ref[HALO - 1:HALO, :].astype(F32) * keep_prev
    next_row = un_ref[0:1, :].astype(F32) * keep_next
    row = lax.broadcasted_iota(I32, u.shape, 0)
    u_m1 = jnp.where(row == 0, prev_row, pltpu.roll(u, 1, 0))
    u_p1 = jnp.where(row == tm - 1, next_row, pltpu.roll(u, tm - 1, 0))
    conv = cw_ref[0:1, :] * u_m1 + cw_ref[1:2, :] * u + cw_ref[2:3, :] * u_p1
    bx = (gb_ref[...].astype(F32) * conv).astype(BF16)
    y = _dot(a_ref[...], wa_ref[...]) + _dot(bx, wb_ref[...])
    _residual_norm_route(x_ref[...], y, g1_ref[...], n2g_ref[...], sh2_ref[...], sc2_ref[...],
                         rwt_ref[...], rb_ref[...], xo_ref, ho_ref, ri_ref, rf_ref, cnt_ref)


def _outproj(a, gb, u, cw, wa, wb, x2, mod, n2g, rwt, rb, tpb):
    n, d = x2.shape
    tm = TMD
    w = NA_WIDTH
    hb = tm // HALO
    nh = n // HALO
    half = pl.BlockSpec((tm, w), lambda t: (t, 0))
    tail_in, out_specs, out_shape = _tail_specs(n, d, tm, mod, 0, tpb)
    in_specs = [half, half, half,
                pl.BlockSpec((HALO, w), lambda t: (jnp.maximum(t * hb - 1, 0), 0)),
                pl.BlockSpec((HALO, w), lambda t: (jnp.minimum((t + 1) * hb, nh - 1), 0)),
                pl.BlockSpec(cw.shape, lambda t: (0, 0)),
                pl.BlockSpec(wa.shape, lambda t: (0, 0)),
                pl.BlockSpec(wb.shape, lambda t: (0, 0))] + tail_in
    return pl.pallas_call(
        functools.partial(_outproj_kernel, tpb),
        grid=(n // tm,),
        in_specs=in_specs, out_specs=out_specs, out_shape=out_shape,
        compiler_params=_cparams(("arbitrary",)),
        name="outproj_route",
    )(a, gb, u, u, u, cw, wa, wb, x2, mod.arr, n2g, mod.arr, mod.arr, rwt, rb)


def _conf_kernel(tpb, u_ref, up_ref, un_ref, dww_ref, dwb_ref, lng_ref, lnb_ref, w2_ref, b2_ref,
                 x_ref, g1_ref, n2g_ref, sh2_ref, sc2_ref, rwt_ref, rb_ref,
                 xo_ref, ho_ref, ri_ref, rf_ref, cnt_ref, ue_ref, sh_ref, conv_ref, wb_ref):
    t = pl.program_id(0)
    tm, d = u_ref.shape
    taps = dww_ref.shape[0]
    sl = SUBLANES_F32

    @pl.when(t == 0)
    def _():
        for k in range(taps):
            wb_ref[k * sl:(k + 1) * sl, :] = jnp.broadcast_to(dww_ref[k:k + 1, :], (sl, d))

    keep_prev = jnp.where(t % tpb == 0, 0.0, 1.0)
    keep_next = jnp.where(t % tpb == tpb - 1, 0.0, 1.0)
    ue_ref[0:HALO, :] = up_ref[...].astype(F32) * keep_prev
    ue_ref[HALO:HALO + tm, :] = u_ref[...].astype(F32)
    ue_ref[HALO + tm:HALO + tm + HALO, :] = un_ref[...].astype(F32) * keep_next
    ext = tm + 2 * HALO - sl
    for s in range(sl):
        sh_ref[s, :, :] = ue_ref[s:s + ext, :]
    off0 = HALO - (taps - 1) // 2
    nv = CONV_RC // sl
    for cb in range(d // LANES):
        cols = slice(cb * LANES, (cb + 1) * LANES)

        def body(rc, carry, cols=cols):
            r0 = pl.multiple_of(rc * CONV_RC, CONV_RC)
            accs = [None] * nv
            for k in range(taps):
                s, a = (k + off0) % sl, (k + off0) // sl
                wk = wb_ref[k * sl:(k + 1) * sl, cols]
                for j in range(nv):
                    term = wk * sh_ref[s, pl.ds(r0 + sl * (a + j), sl), cols]
                    accs[j] = term if accs[j] is None else accs[j] + term
            for j in range(nv):
                conv_ref[pl.ds(r0 + sl * j, sl), cols] = accs[j]
            return carry

        lax.fori_loop(0, tm // CONV_RC, body, 0)
    c = conv_ref[...] + dwb_ref[...]
    mu = jnp.mean(c, axis=-1, keepdims=True)
    cc = c - mu
    var = jnp.mean(cc * cc, axis=-1, keepdims=True)
    z = cc * lax.rsqrt(var + EPS) * lng_ref[...] + lnb_ref[...]
    z = (z * _sigmoid(z)).astype(BF16)
    y = _dot(z, w2_ref[...]) + b2_ref[...]
    _residual_norm_route(x_ref[...], y, g1_ref[...], n2g_ref[...], sh2_ref[...], sc2_ref[...],
                         rwt_ref[...], rb_ref[...], xo_ref, ho_ref, ri_ref, rf_ref, cnt_ref)


def _conf(u, dww, dwb, lng, lnb, w2, b2, x2, mod, n2g, rwt, rb, tpb):
    n, d = x2.shape
    tm = TMD
    hb = tm // HALO
    nh = n // HALO
    taps = dww.shape[0]
    assert (taps - 1) // 2 <= HALO - 1 and taps // 2 <= HALO
    one = pl.BlockSpec((1, d), lambda t: (0, 0))
    tail_in, out_specs, out_shape = _tail_specs(n, d, tm, mod, 1, tpb)
    in_specs = [pl.BlockSpec((tm, d), lambda t: (t, 0)),
                pl.BlockSpec((HALO, d), lambda t: (jnp.maximum(t * hb - 1, 0), 0)),
                pl.BlockSpec((HALO, d), lambda t: (jnp.minimum((t + 1) * hb, nh - 1), 0)),
                pl.BlockSpec(dww.shape, lambda t: (0, 0)), one, one, one,
                pl.BlockSpec(w2.shape, lambda t: (0, 0)), one] + tail_in
    ext = tm + 2 * HALO - SUBLANES_F32
    return pl.pallas_call(
        functools.partial(_conf_kernel, tpb),
        grid=(n // tm,),
        in_specs=in_specs, out_specs=out_specs, out_shape=out_shape,
        scratch_shapes=[pltpu.VMEM((tm + 2 * HALO, d), F32),
                        pltpu.VMEM((SUBLANES_F32, ext, d), F32),
                        pltpu.VMEM((tm, d), F32),
                        pltpu.VMEM((taps * SUBLANES_F32, d), F32)],
        compiler_params=_cparams(("arbitrary",)),
        name="conf_conv_route",
    )(u, u, u, dww, dwb, lng, lnb, w2, b2, x2, mod.arr, n2g, mod.arr, mod.arr, rwt, rb)


def _sorted_rows(n):
    t_n = n // TMD
    rows = 2 * n + (CHUNK - 1) * t_n * N_EXPERTS + N_EXPERTS * (TME - CHUNK)
    return (rows + TME - 1) // TME * TME


def _moe_plan(cnt, n):
    t_n, e_n = cnt.shape
    rt = _sorted_rows(n)
    seg = (cnt + CHUNK - 1) // CHUNK * CHUNK
    lo = jnp.cumsum(seg, axis=1) - seg
    etot = jnp.sum(seg, axis=0)
    epad = (etot + TME - 1) // TME * TME
    eend = jnp.cumsum(epad)
    estart = eend - epad
    go = estart[None, :] + jnp.cumsum(seg, axis=0) - seg
    nchunk = jnp.sum(seg, axis=1) // CHUNK
    c_rows = jnp.arange(CMAX, dtype=I32) * CHUNK
    e_ids = jnp.arange(e_n, dtype=I32)
    owner = jnp.sum(((lo + seg)[:, None, :] <= c_rows[None, :, None]).astype(I32), axis=-1)
    owner = jnp.minimum(owner, e_n - 1)
    shift = jnp.sum(jnp.where(owner[:, :, None] == e_ids, (go - lo)[:, None, :], 0), axis=-1)
    cmap = jnp.clip((shift + c_rows[None, :]) // CHUNK, 0, rt // CHUNK - 1)
    npad = (epad - etot) // CHUNK
    pend = jnp.cumsum(npad)
    kk = jnp.arange(e_n * (TME // CHUNK - 1), dtype=I32)
    pown = jnp.minimum(jnp.sum((pend[None, :] <= kk[:, None]).astype(I32), axis=-1), e_n - 1)
    pbase = (estart + etot) // CHUNK - (pend - npad)
    padmap = kk + jnp.sum(jnp.where(pown[:, None] == e_ids, pbase[None, :], 0), axis=-1)
    padmap = jnp.clip(padmap, 0, rt // CHUNK - 1)
    tile_row = jnp.arange(rt // TME, dtype=I32) * TME
    texp = jnp.minimum(jnp.sum((eend[None, :] <= tile_row[:, None]).astype(I32), axis=-1), e_n - 1)
    as_i32 = lambda a: a.astype(I32)
    return dict(lo=as_i32(lo.reshape(-1)), nchunk=as_i32(nchunk), cmap=as_i32(cmap.reshape(-1)),
                padmap=as_i32(padmap), npadtot=as_i32(pend[-1:]), texp=as_i32(texp),
                nact=as_i32(eend[-1:] // TME))


def _chunk_copy(src_ref, dst_ref, sem):
    return pltpu.make_async_copy(src_ref, dst_ref, sem)


def _wait_chunks(n_chunks, desc):
    for bit in range(CMAX.bit_length()):
        @pl.when(((n_chunks >> bit) & 1) == 1)
        def _(bit=bit):
            desc(CHUNK << bit).wait()


def _dispatch_kernel(nchunk_ref, cmap_ref, npad_ref, padmap_ref, nact_ref, lo_ref,
                     h_ref, ri_ref, xs_ref, lp_ref, xc_ref, z_ref, sem, zsem):
    t = pl.program_id(0)
    nt = pl.num_programs(0)
    slot = t % 2
    ntail = xs_ref.shape[0] // TME - nact_ref[0]

    def pad_copy(k):
        g = pl.multiple_of(padmap_ref[k] * CHUNK, CHUNK)
        return _chunk_copy(z_ref.at[0:CHUNK, :], xs_ref.at[pl.ds(g, CHUNK), :], zsem.at[0])

    def tail_copy(k):
        g = pl.multiple_of((nact_ref[0] + k) * TME, TME)
        return _chunk_copy(z_ref, xs_ref.at[pl.ds(g, TME), :], zsem.at[0])

    @pl.when(t == 0)
    def _():
        z_ref[...] = jnp.zeros(z_ref.shape, z_ref.dtype)
        lax.fori_loop(0, npad_ref[0], lambda k, c: (pad_copy(k).start(), c)[1], 0)
        lax.fori_loop(0, ntail, lambda k, c: (tail_copy(k).start(), c)[1], 0)

    e0, e1 = ri_ref[0:1, :], ri_ref[1:2, :]
    b0 = jnp.zeros(e0.shape, I32)
    b1 = jnp.zeros(e0.shape, I32)
    for e in range(N_EXPERTS):
        lo_e = lo_ref[t * N_EXPERTS + e]
        b0 = jnp.where(e0 == e, lo_e, b0)
        b1 = jnp.where(e1 == e, lo_e, b1)
    lp0 = b0 + ri_ref[2:3, :]
    lp1 = b1 + ri_ref[3:4, :]
    lp_ref[...] = jnp.zeros(lp_ref.shape, I32)
    lp_ref[0:1, :] = lp0
    lp_ref[1:2, :] = lp1

    rows = lax.broadcasted_iota(I32, (RCOMP, TMD), 0)
    onehot = jnp.where((lp0 == rows) | (lp1 == rows), 1.0, 0.0).astype(BF16)
    xc_ref[slot] = _dot(onehot, h_ref[...]).astype(BF16)

    def out_copy(tt, sl, c):
        g = pl.multiple_of(cmap_ref[tt * CMAX + c] * CHUNK, CHUNK)
        l = pl.multiple_of(c * CHUNK, CHUNK)
        return _chunk_copy(xc_ref.at[sl, pl.ds(l, CHUNK), :], xs_ref.at[pl.ds(g, CHUNK), :], sem.at[sl])

    lax.fori_loop(0, nchunk_ref[t], lambda c, cr: (out_copy(t, slot, c).start(), cr)[1], 0)

    def out_desc(sl):
        return lambda rows: _chunk_copy(xc_ref.at[sl, 0:rows, :], xs_ref.at[0:rows, :], sem.at[sl])

    @pl.when(t > 0)
    def _():
        _wait_chunks(nchunk_ref[t - 1], out_desc(1 - slot))

    @pl.when(t == 0)
    def _():
        lax.fori_loop(0, npad_ref[0], lambda k, c: (pad_copy(k).wait(), c)[1], 0)
        lax.fori_loop(0, ntail, lambda k, c: (tail_copy(k).wait(), c)[1], 0)

    @pl.when(t == nt - 1)
    def _():
        _wait_chunks(nchunk_ref[t], out_desc(slot))


def _dispatch(h2, ri, plan):
    n, d = h2.shape
    t_n = n // TMD
    rt = _sorted_rows(n)
    grid_spec = pltpu.PrefetchScalarGridSpec(
        num_scalar_prefetch=6,
        grid=(t_n,),
        in_specs=[pl.BlockSpec((TMD, d), lambda t, *_: (t, 0)),
                  pl.BlockSpec((SUBLANES_F32, TMD), lambda t, *_: (0, t))],
        out_specs=[pl.BlockSpec(memory_space=pl.ANY),
                   pl.BlockSpec((None, SUBLANES_F32, TMD), lambda t, *_: (t, 0, 0))],
        scratch_shapes=[pltpu.VMEM((2, RCOMP, d), BF16),
                        pltpu.VMEM((TME, d), BF16),
                        pltpu.SemaphoreType.DMA((2,)),
                        pltpu.SemaphoreType.DMA((1,))],
    )
    return pl.pallas_call(
        _dispatch_kernel,
        grid_spec=grid_spec,
        out_shape=[jax.ShapeDtypeStruct((rt, d), BF16),
                   jax.ShapeDtypeStruct((t_n, SUBLANES_F32, TMD), I32)],
        compiler_params=_cparams(("arbitrary",)),
        name="moe_dispatch",
    )(plan["nchunk"], plan["cmap"], plan["npadtot"], plan["padmap"], plan["nact"], plan["lo"], h2, ri)


def _expert_kernel(texp_ref, nact_ref, x_ref, w1_ref, w3_ref, w2_ref, y_ref, w1b, w3b, w2b):
    i = pl.program_id(0)

    @pl.when(i < nact_ref[0])
    def _():
        prev = texp_ref[jnp.maximum(i - 1, 0)]

        @pl.when((i == 0) | (texp_ref[i] != prev))
        def _():
            w1b[...] = w1_ref[...].astype(BF16)
            w3b[...] = w3_ref[...].astype(BF16)
            w2b[...] = w2_ref[...].astype(BF16)

        x = x_ref[...]
        h = _dot(x, w1b[...])
        g = _dot(x, w3b[...])
        a = (h * _sigmoid(h) * g).astype(BF16)
        y_ref[...] = _dot(a, w2b[...]).astype(y_ref.dtype)

    @pl.when(i >= nact_ref[0])
    def _():
        y_ref[...] = jnp.zeros(y_ref.shape, y_ref.dtype)


def _experts(xs, w1, w3, w2, layer, plan):
    rt, d = xs.shape
    f = w1.shape[-1]
    nt = rt // TME

    def row_map(i, texp, nact):
        return (jnp.clip(i, 0, jnp.maximum(nact[0] - 1, 0)), 0)

    def w_map(i, texp, nact):
        return (layer, texp[jnp.clip(i, 0, jnp.maximum(nact[0] - 1, 0))], 0, 0)

    grid_spec = pltpu.PrefetchScalarGridSpec(
        num_scalar_prefetch=2,
        grid=(nt,),
        in_specs=[pl.BlockSpec((TME, d), row_map),
                  pl.BlockSpec((None, None, d, f), w_map),
                  pl.BlockSpec((None, None, d, f), w_map),
                  pl.BlockSpec((None, None, f, d), w_map)],
        out_specs=pl.BlockSpec((TME, d), lambda i, texp, nact: (i, 0)),
        scratch_shapes=[pltpu.VMEM((d, f), BF16), pltpu.VMEM((d, f), BF16), pltpu.VMEM((f, d), BF16)],
    )
    return pl.pallas_call(
        _expert_kernel,
        grid_spec=grid_spec,
        out_shape=jax.ShapeDtypeStruct((rt, d), BF16),
        compiler_params=_cparams(("arbitrary",)),
        name="moe_experts",
    )(plan["texp"], plan["nact"], xs, w1, w3, w2)


def _combine_core(nchunk_ref, cmap_ref, ys_ref, ci_ref, cf_ref, x_ref, g2_ref, yc_ref, sem):
    t = pl.program_id(0)
    nt = pl.num_programs(0)
    slot = t % 2

    def in_copy(tt, sl, c):
        g = pl.multiple_of(cmap_ref[tt * CMAX + c] * CHUNK, CHUNK)
        l = pl.multiple_of(c * CHUNK, CHUNK)
        return _chunk_copy(ys_ref.at[pl.ds(g, CHUNK), :], yc_ref.at[sl, pl.ds(l, CHUNK), :], sem.at[sl])

    @pl.when(t == 0)
    def _():
        yc_ref[...] = jnp.zeros(yc_ref.shape, yc_ref.dtype)
        lax.fori_loop(0, nchunk_ref[0], lambda c, cr: (in_copy(0, 0, c).start(), cr)[1], 0)

    @pl.when(t + 1 < nt)
    def _():
        lax.fori_loop(0, nchunk_ref[t + 1], lambda c, cr: (in_copy(t + 1, 1 - slot, c).start(), cr)[1], 0)

    _wait_chunks(nchunk_ref[t],
                 lambda rows: _chunk_copy(ys_ref.at[0:rows, :], yc_ref.at[slot, 0:rows, :], sem.at[slot]))

    ci = ci_ref[...]
    cf = cf_ref[...]
    cols = lax.broadcasted_iota(I32, (TMD, RCOMP), 1)
    w = jnp.where(ci[:, 0:1] == cols, cf[:, 0:1], 0.0) + jnp.where(ci[:, 1:2] == cols, cf[:, 1:2], 0.0)
    moe = _dot(w.astype(BF16), yc_ref[slot])
    return x_ref[...] + g2_ref[...] * moe


def _combine_final_kernel(nchunk_ref, cmap_ref, ys_ref, ci_ref, cf_ref, x_ref, g2_ref, fg_ref,
                          o_ref, yc_ref, sem):
    x1 = _combine_core(nchunk_ref, cmap_ref, ys_ref, ci_ref, cf_ref, x_ref, g2_ref, yc_ref, sem)
    ms = jnp.mean(x1 * x1, axis=-1, keepdims=True)
    o_ref[...] = x1 * lax.rsqrt(ms + EPS) * fg_ref[...]


def _combine_pw1_kernel(nchunk_ref, cmap_ref, ys_ref, ci_ref, cf_ref, x_ref, g2_ref,
                        n1g_ref, sh_ref, sc_ref, w_ref, b_ref, o_ref, u_ref, yc_ref, sem):
    x1 = _combine_core(nchunk_ref, cmap_ref, ys_ref, ci_ref, cf_ref, x_ref, g2_ref, yc_ref, sem)
    o_ref[...] = x1
    d = x1.shape[1]
    h = _norm_mod(x1, n1g_ref[...], sh_ref[...], sc_ref[...]).astype(BF16)
    a = _dot(h, w_ref[:, 0:d]) + b_ref[:, 0:d]
    g = _dot(h, w_ref[:, d:2 * d]) + b_ref[:, d:2 * d]
    u_ref[...] = (a * _sigmoid(g)).astype(BF16)


def _combine(ys, plan, lp, rf, x2, mod, layer, tpb, tail):
    n, d = x2.shape
    t_n = n // TMD
    col = pl.BlockSpec((TMD, SUBLANES_F32), lambda t, *_: (t, 0))
    one = pl.BlockSpec((1, d), lambda t, *_: (0, 0))
    tile = pl.BlockSpec((TMD, d), lambda t, *_: (t, 0))
    in_specs = [pl.BlockSpec(memory_space=pl.ANY), col, col, tile, mod.vec(layer, 5, tpb)]
    args = [plan["nchunk"], plan["cmap"], ys,
            jnp.transpose(lp, (0, 2, 1)).reshape(n, SUBLANES_F32), rf.T, x2, mod.arr]
    if tail[0] == "final":
        body, name = _combine_final_kernel, "moe_combine_final"
        in_specs += [one]
        args += [tail[1]]
        out_specs = tile
        out_shape = jax.ShapeDtypeStruct((n, d), F32)
    else:
        body, name = _combine_pw1_kernel, "moe_combine_pw1"
        _, n1g, w_bf, bias = tail
        in_specs += [one, mod.vec(layer + 1, 0, tpb), mod.vec(layer + 1, 1, tpb),
                     pl.BlockSpec(w_bf.shape, lambda t, *_: (0, 0)),
                     pl.BlockSpec(bias.shape, lambda t, *_: (0, 0))]
        args += [n1g, mod.arr, mod.arr, w_bf, bias]
        out_specs = [tile, tile]
        out_shape = [jax.ShapeDtypeStruct((n, d), F32), jax.ShapeDtypeStruct((n, d), BF16)]
    grid_spec = pltpu.PrefetchScalarGridSpec(
        num_scalar_prefetch=2,
        grid=(t_n,),
        in_specs=in_specs,
        out_specs=out_specs,
        scratch_shapes=[pltpu.VMEM((2, RCOMP, d), BF16), pltpu.SemaphoreType.DMA((2,))],
    )
    return pl.pallas_call(body, grid_spec=grid_spec, out_shape=out_shape,
                          compiler_params=_cparams(("arbitrary",)), name=name)(*args)


def _moe(x1, h2, ri, rf, cnt, w1, w3, w2, layer, mod, tpb, tail):
    n = x1.shape[0]
    plan = _moe_plan(cnt[:, :, 0], n)
    xs, lp = _dispatch(h2, ri, plan)
    ys = _experts(xs, w1, w3, w2, layer, plan)
    return _combine(ys, plan, lp, rf, x1, mod, layer, tpb, tail)


def kernel(x, c, ctx, c_ctx, norm1_g, norm2_g, w_mod, b_mod, na_w_in, na_rpb, sc_conv_w, ab_w_out,
           cf_pw1_w, cf_pw1_b, cf_dw_w, cf_dw_b, cf_ln_g, cf_ln_b, cf_pw2_w, cf_pw2_b,
           router_w, router_b, moe_w1, moe_w3, moe_w2, final_g):
    b, s, d = x.shape
    l = ctx.shape[1]
    n = b * s
    depth = w_mod.shape[0]
    assert depth == 2, "layer pattern (attention layer, Conformer layer) is written out for depth 2"
    assert s % TM_PROJ == 0 and s % TMD == 0
    rows = s // GRID_W
    assert rows % ATT_ROWS == 0 and rows >= ATT_WIN and s % ATT_KBLK == 0
    tpb = s // TM_PROJ
    tpb_d = s // TMD

    mr = (b + 1 + SUBLANES_F32 - 1) // SUBLANES_F32 * SUBLANES_F32
    c_all = jnp.concatenate([c, c_ctx[None, :], jnp.zeros((mr - b - 1, d), F32)], axis=0)
    mod = _Mod(_modulation(c_all, w_mod, b_mod), d)

    x2 = x.reshape(n, d)
    rwt = router_w.T
    rb = router_b.reshape(N_EXPERTS, 1)
    fg = final_g.reshape(1, d)

    w_in = na_w_in[0].astype(BF16)
    n1g = norm1_g[0].reshape(1, d)
    wkt = w_in[:, NA_WIDTH:2 * NA_WIDTH].T
    q, kt, v, gb, u = _inproj(x2, n1g, mod, w_in, wkt, tpb)
    kct, vc = _ctxproj(ctx.reshape(b * l, d), n1g, mod, b, wkt, w_in[:, 2 * NA_WIDTH:3 * NA_WIDTH], l)
    tab = _attn_bias_tables(na_rpb[0], rows)
    a = _attention(q, kt, v, kct, vc, tab, b, s, l)
    w_out = ab_w_out[0].astype(BF16)
    x1, h2, ri, rf, cnt = _outproj(a, gb, u, sc_conv_w[0], w_out[:NA_WIDTH], w_out[NA_WIDTH:], x2,
                                   mod, norm2_g[0].reshape(1, d), rwt, rb, tpb_d)
    x2, u = _moe(x1, h2, ri, rf, cnt, moe_w1, moe_w3, moe_w2, 0, mod, tpb_d,
                 ("pw1", norm1_g[1].reshape(1, d), cf_pw1_w[0].astype(BF16), cf_pw1_b[0].reshape(1, 2 * d)))
    x1, h2, ri, rf, cnt = _conf(u, cf_dw_w[0], cf_dw_b[0].reshape(1, d), cf_ln_g[0].reshape(1, d),
                                cf_ln_b[0].reshape(1, d), cf_pw2_w[0].astype(BF16),
                                cf_pw2_b[0].reshape(1, d), x2, mod, norm2_g[1].reshape(1, d),
                                rwt, rb, tpb_d)
    out = _moe(x1, h2, ri, rf, cnt, moe_w1, moe_w3, moe_w2, 1, mod, tpb_d, ("final", fg))
    return out.reshape(b, s, d)
```

```python
import functools

import numpy as np
import jax
import jax.numpy as jnp
from jax import lax
from jax.experimental import pallas as pl
from jax.experimental.pallas import tpu as pltpu

F32 = jnp.float32
BF16 = jnp.bfloat16
I32 = jnp.int32

GRID_W = 64
NA_HEADS = 8
NA_HEAD_DIM = 64
NA_WIDTH = NA_HEADS * NA_HEAD_DIM
NA_KR = 8
NA_KC = 16
N_EXPERTS = 16
N_GROUPS = 4
EXPERTS_PER_GROUP = N_EXPERTS // N_GROUPS
N_MOD = 6
EPS = 1e-6
NEG = -1e30
LOG2E = 1.4426950408889634

LANES = 128
SUBLANES_F32 = 8
SUBLANES_BF16 = 16

TM_PROJ = 1024
ATT_ROWS = 8
ATT_WIN = 2 * ATT_ROWS
ATT_KBLK = 256
ATT_HEADS = 4
TMD = 512
CHUNK = SUBLANES_BF16
CMAX = (2 * TMD + (N_EXPERTS - 1) * CHUNK + CHUNK - 1) // CHUNK + 1
RCOMP = CMAX * CHUNK
TME = 1024
CONV_RC = 128
HALO = SUBLANES_BF16
VMEM_LIMIT = 52 * 1024 * 1024


def _cparams(sem):
    return pltpu.CompilerParams(dimension_semantics=sem, vmem_limit_bytes=VMEM_LIMIT)


def _dot(a, b):
    return jnp.dot(a, b, preferred_element_type=F32)


def _dot_nt(a, b, precision=None):
    return lax.dot_general(a, b, (((1,), (1,)), ((), ())), precision=precision,
                           preferred_element_type=F32)


def _sigmoid(x):
    return 1.0 / (1.0 + jnp.exp(-x))


def _norm_mod(x, g, sh, sc):
    ms = jnp.mean(x * x, axis=-1, keepdims=True)
    y = x * lax.rsqrt(ms + EPS) * g
    return y * (1.0 + sc) + sh


def _split_bf16(a):
    hi = a.astype(BF16)
    lo = (a - hi.astype(F32)).astype(BF16)
    return hi, lo


def _mod_kernel(c_ref, w_ref, b_ref, o_ref):
    c = c_ref[...]
    mr = c.shape[0]
    hi, lo = _split_bf16(c * _sigmoid(c))
    w_hi, w_lo = _split_bf16(w_ref[...])
    r = _dot(jnp.concatenate([hi, lo], axis=0), w_hi)
    o_ref[...] = r[:mr] + r[mr:] + _dot(hi, w_lo) + b_ref[...]


def _modulation(c_all, w_mod, b_mod):
    depth, d, n6 = w_mod.shape
    mr = c_all.shape[0]
    tn = 1024
    return pl.pallas_call(
        _mod_kernel,
        grid=(depth, n6 // tn),
        in_specs=[pl.BlockSpec((mr, d), lambda i, j: (0, 0)),
                  pl.BlockSpec((None, d, tn), lambda i, j: (i, 0, j)),
                  pl.BlockSpec((None, 1, tn), lambda i, j: (i, 0, j))],
        out_specs=pl.BlockSpec((None, mr, tn), lambda i, j: (i, 0, j)),
        out_shape=jax.ShapeDtypeStruct((depth, mr, n6), F32),
        compiler_params=_cparams(("arbitrary", "arbitrary")),
        name="modulation",
    )(c_all, w_mod, b_mod.reshape(depth, 1, n6))


class _Mod:
    def __init__(self, mod, d):
        self.depth, self.mr, _ = mod.shape
        self.d = d
        self.arr = mod.reshape(self.depth * self.mr * N_MOD, 1, d)

    def vec(self, layer, j, tpb):
        mr = self.mr
        return pl.BlockSpec((None, 1, self.d),
                            lambda t, *_: ((layer * mr + t // tpb) * N_MOD + j, 0, 0))

    def row(self, layer, j, r):
        mr = self.mr
        return pl.BlockSpec((None, 1, self.d), lambda t, *_: ((layer * mr + r) * N_MOD + j, 0, 0))


def _inproj_kernel(x_ref, g_ref, sh_ref, sc_ref, w_ref, wkt_ref, q_ref, kt_ref, v_ref, gb_ref, u_ref):
    h = _norm_mod(x_ref[...], g_ref[...], sh_ref[...], sc_ref[...]).astype(BF16)
    w = NA_WIDTH
    q_ref[...] = (_dot(h, w_ref[:, 0:w]) * (NA_HEAD_DIM ** -0.5 * LOG2E)).astype(BF16)
    kt_ref[...] = _dot_nt(wkt_ref[...], h).astype(BF16)
    v_ref[...] = _dot(h, w_ref[:, 2 * w:3 * w]).astype(BF16)
    gb_ref[...] = _dot(h, w_ref[:, 3 * w:4 * w]).astype(BF16)
    u_ref[...] = (_dot(h, w_ref[:, 4 * w:5 * w]) * _dot(h, w_ref[:, 5 * w:6 * w])).astype(BF16)


def _inproj(x2, g, mod, w_bf, wkt_bf, tpb):
    n, d = x2.shape
    tm = TM_PROJ
    out = jax.ShapeDtypeStruct((n, NA_WIDTH), BF16)
    ospec = pl.BlockSpec((tm, NA_WIDTH), lambda t: (t, 0))
    out_t = jax.ShapeDtypeStruct((NA_WIDTH, n), BF16)
    ospec_t = pl.BlockSpec((NA_WIDTH, tm), lambda t: (0, t))
    return pl.pallas_call(
        _inproj_kernel,
        grid=(n // tm,),
        in_specs=[pl.BlockSpec((tm, d), lambda t: (t, 0)),
                  pl.BlockSpec((1, d), lambda t: (0, 0)), mod.vec(0, 0, tpb), mod.vec(0, 1, tpb),
                  pl.BlockSpec(w_bf.shape, lambda t: (0, 0)),
                  pl.BlockSpec(wkt_bf.shape, lambda t: (0, 0))],
        out_specs=[ospec, ospec_t, ospec, ospec, ospec],
        out_shape=[out, out_t, out, out, out],
        compiler_params=_cparams(("arbitrary",)),
        name="inproj",
    )(x2, g, mod.arr, mod.arr, w_bf, wkt_bf)


def _ctxproj_kernel(x_ref, g_ref, sh_ref, sc_ref, wkt_ref, wv_ref, kt_ref, v_ref):
    h = _norm_mod(x_ref[...], g_ref[...], sh_ref[...], sc_ref[...]).astype(BF16)
    kt_ref[...] = _dot_nt(wkt_ref[...], h).astype(BF16)
    v_ref[...] = _dot(h, wv_ref[...]).astype(BF16)


def _ctxproj(ctx2, g, mod, ctx_row, wkt_bf, wv_bf, l):
    n, d = ctx2.shape
    return pl.pallas_call(
        _ctxproj_kernel,
        grid=(n // l,),
        in_specs=[pl.BlockSpec((l, d), lambda t: (t, 0)), pl.BlockSpec((1, d), lambda t: (0, 0)),
                  mod.row(0, 0, ctx_row), mod.row(0, 1, ctx_row),
                  pl.BlockSpec(wkt_bf.shape, lambda t: (0, 0)),
                  pl.BlockSpec(wv_bf.shape, lambda t: (0, 0))],
        out_specs=[pl.BlockSpec((NA_WIDTH, l), lambda t: (0, t)),
                   pl.BlockSpec((l, NA_WIDTH), lambda t: (t, 0))],
        out_shape=[jax.ShapeDtypeStruct((NA_WIDTH, n), BF16), jax.ShapeDtypeStruct((n, NA_WIDTH), BF16)],
        compiler_params=_cparams(("arbitrary",)),
        name="ctxproj",
    )(ctx2, g, mod.arr, mod.arr, wkt_bf, wv_bf)


def _bias_row_index(rows):
    nb = rows // ATT_ROWS
    n_dr = 2 * NA_KR - 1
    dr_idx = np.full((3, ATT_ROWS, ATT_WIN), n_dr, np.int32)
    for ty, j in enumerate((0, 1, nb - 1)):
        ws = ATT_ROWS * j - NA_KR // 2
        for qr in range(ATT_ROWS):
            r = ATT_ROWS * j + qr
            rs = int(np.clip(r - NA_KR // 2, 0, rows - NA_KR))
            for kr in range(ATT_WIN):
                key_row = ws + kr
                if rs <= key_row < rs + NA_KR:
                    dr_idx[ty, qr, kr] = key_row - r + (NA_KR - 1)
    kr_blk = ATT_KBLK // GRID_W
    assert (dr_idx[:, :ATT_ROWS // 2, ATT_WIN - kr_blk:] == n_dr).all()
    assert (dr_idx[:, ATT_ROWS // 2:, :kr_blk] == n_dr).all()
    return dr_idx


def _bias_kernel(dr_idx, rpb_ref, out_ref, t_ref):
    h = pl.program_id(0)
    n_dr = 2 * NA_KR - 1
    n_dc = 2 * NA_KC - 1
    qc = lax.broadcasted_iota(I32, (GRID_W, GRID_W), 0)
    kc = lax.broadcasted_iota(I32, (GRID_W, GRID_W), 1)
    cs = jnp.clip(qc - NA_KC // 2, 0, GRID_W - NA_KC)
    valid = (kc >= cs) & (kc < cs + NA_KC)
    dc = jnp.where(valid, kc - qc + (NA_KC - 1), -1)
    for dr in range(n_dr):
        slab = jnp.full((GRID_W, GRID_W), NEG, F32)
        for m in range(n_dc):
            slab = jnp.where(dc == m, rpb_ref[(h * n_dr + dr) * n_dc + m] * LOG2E, slab)
        t_ref[dr] = slab
    t_ref[n_dr] = jnp.full((GRID_W, GRID_W), NEG, F32)
    for ty in range(3):
        for qr in range(ATT_ROWS):
            for kr in range(ATT_WIN):
                out_ref[ty, qr * GRID_W:(qr + 1) * GRID_W, kr * GRID_W:(kr + 1) * GRID_W] = \
                    t_ref[int(dr_idx[ty, qr, kr])]


def _attn_bias_tables(rpb, rows):
    h = rpb.shape[0]
    dr_idx = _bias_row_index(rows)
    qn, kn = ATT_ROWS * GRID_W, ATT_WIN * GRID_W
    return pl.pallas_call(
        functools.partial(_bias_kernel, dr_idx),
        grid=(h,),
        in_specs=[pl.BlockSpec(memory_space=pltpu.SMEM)],
        out_specs=pl.BlockSpec((None, 3, qn, kn), lambda i: (i, 0, 0, 0)),
        out_shape=jax.ShapeDtypeStruct((h, 3, qn, kn), F32),
        scratch_shapes=[pltpu.VMEM((2 * NA_KR, GRID_W, GRID_W), F32)],
        compiler_params=_cparams(("arbitrary",)),
        name="attn_bias_table",
    )(rpb.astype(F32).reshape(-1))


def _attn_kernel(q_ref, k0_ref, k1_ref, k2_ref, k3_ref, kc_ref,
                 v0_ref, v1_ref, v2_ref, v3_ref, vc_ref, tab_ref, o_ref):
    lane = lax.broadcasted_iota(I32, (1, LANES), 1)
    kt_refs = [k0_ref, k1_ref, k2_ref, k3_ref, kc_ref]
    v_refs = [v0_ref, v1_ref, v2_ref, v3_ref, vc_ref]
    kb = ATT_KBLK
    half = q_ref.shape[0] // 2
    hms = [jnp.where((lane >= hh * NA_HEAD_DIM) & (lane < (hh + 1) * NA_HEAD_DIM), 1.0, 0.0).astype(BF16)
           for hh in range(2)]
    hmf = [hm.astype(F32) for hm in hms]
    for pp in range(ATT_HEADS // 2):
        cols = slice(pp * LANES, (pp + 1) * LANES)
        kt = [r[cols, :] for r in kt_refs]
        vv = [r[:, cols] for r in v_refs]
        for r0, blocks in ((0, (0, 1, 2)), (half, (1, 2, 3))):
            q = q_ref[r0:r0 + half, cols]
            qs = jnp.concatenate([q * hms[0], q * hms[1]], axis=0)
            s = [_dot(qs, kt[i])
                 + jnp.concatenate([tab_ref[2 * pp, r0:r0 + half, i * kb:(i + 1) * kb],
                                    tab_ref[2 * pp + 1, r0:r0 + half, i * kb:(i + 1) * kb]], axis=0)
                 for i in blocks]
            s.append(_dot(qs, kt[4]))
            m = jnp.max(s[0], axis=-1, keepdims=True)
            for si in s[1:]:
                m = jnp.maximum(m, jnp.max(si, axis=-1, keepdims=True))
            p = [jnp.exp2(si - m) for si in s]
            l = jnp.sum(p[0], axis=-1, keepdims=True)
            for pi in p[1:]:
                l = l + jnp.sum(pi, axis=-1, keepdims=True)
            o = _dot(p[3].astype(BF16), vv[4])
            for n_, i in enumerate(blocks):
                o = o + _dot(p[n_].astype(BF16), vv[i])
            o = o / l
            acc = o[:half] * hmf[0] + o[half:] * hmf[1]
            o_ref[r0:r0 + half, cols] = acc.astype(o_ref.dtype)


def _attention(q, kt, v, kct, vc, tab, b, s, l):
    n = q.shape[0]
    rows = s // GRID_W
    nb = rows // ATT_ROWS
    qblk = ATT_ROWS * GRID_W
    kpb = s // ATT_KBLK
    hg = NA_HEADS // ATT_HEADS
    hw = ATT_HEADS * NA_HEAD_DIM

    kper = ATT_ROWS * GRID_W // ATT_KBLK
    koff = (NA_KR // 2) * GRID_W // ATT_KBLK

    def kblk(j, i):
        return jnp.clip(kper * j - koff + i, 0, kpb - 1)

    qspec = pl.BlockSpec((qblk, hw), lambda h, j, bb: (bb * nb + j, h))
    ktspecs = [pl.BlockSpec((hw, ATT_KBLK), lambda h, j, bb, i=i: (h, bb * kpb + kblk(j, i)))
               for i in range(4)]
    vspecs = [pl.BlockSpec((ATT_KBLK, hw), lambda h, j, bb, i=i: (bb * kpb + kblk(j, i), h))
              for i in range(4)]
    ctspec = pl.BlockSpec((hw, l), lambda h, j, bb: (h, bb))
    cspec = pl.BlockSpec((l, hw), lambda h, j, bb: (bb, h))
    tspec = pl.BlockSpec((ATT_HEADS, None, qblk, ATT_WIN * GRID_W),
                         lambda h, j, bb: (h, jnp.where(j == 0, 0, jnp.where(j == nb - 1, 2, 1)), 0, 0))
    return pl.pallas_call(
        _attn_kernel,
        grid=(hg, nb, b),
        in_specs=[qspec] + ktspecs + [ctspec] + vspecs + [cspec, tspec],
        out_specs=qspec,
        out_shape=jax.ShapeDtypeStruct((n, NA_WIDTH), BF16),
        compiler_params=_cparams(("arbitrary", "arbitrary", "arbitrary")),
        name="nbr_attention",
    )(q, kt, kt, kt, kt, kct, v, v, v, v, vc, tab)


def _residual_norm_route(x, y, g1, n2g, sh2, sc2, rwt, rb, xo_ref, ho_ref, ri_ref, rf_ref, cnt_ref):
    tm = x.shape[0]
    x1 = x + g1 * y
    xo_ref[...] = x1
    h2 = _norm_mod(x1, n2g, sh2, sc2)
    h2_hi, h2_lo = _split_bf16(h2)
    ho_ref[...] = h2_hi
    w_hi, w_lo = _split_bf16(rwt)
    lg = _dot_nt(jnp.concatenate([w_hi, w_lo], axis=0), h2_hi)
    logits = lg[:N_EXPERTS] + lg[N_EXPERTS:] + _dot_nt(w_hi, h2_lo)
    scores = _sigmoid(logits)
    sel = scores + rb
    r = [sel[e:e + 1, :] for e in range(N_EXPERTS)]
    sc = [scores[e:e + 1, :] for e in range(N_EXPERTS)]
    grp = []
    for g in range(N_GROUPS):
        a, b_, c, d = r[4 * g:4 * g + 4]
        hi1, lo1 = jnp.maximum(a, b_), jnp.minimum(a, b_)
        hi2, lo2 = jnp.maximum(c, d), jnp.minimum(c, d)
        m1 = jnp.maximum(hi1, hi2)
        m2 = jnp.maximum(jnp.minimum(hi1, hi2), jnp.maximum(lo1, lo2))
        grp.append(m1 + m2)
    best = grp[0]
    gi = jnp.zeros(best.shape, I32)
    for g in range(1, N_GROUPS):
        upd = grp[g] > best
        best = jnp.where(upd, grp[g], best)
        gi = jnp.where(upd, g, gi)
    chosen = []
    for e in range(N_EXPERTS):
        g = e // EXPERTS_PER_GROUP
        rank = jnp.zeros(best.shape, I32)
        for e2 in range(EXPERTS_PER_GROUP * g, EXPERTS_PER_GROUP * (g + 1)):
            if e2 == e:
                continue
            ahead = (r[e2] > r[e]) | ((r[e2] == r[e]) & (e2 < e))
            rank = rank + jnp.where(ahead, 1, 0)
        chosen.append((gi == g) & (rank < 2))
    wsel = [jnp.where(chosen[e], sc[e], 0.0) for e in range(N_EXPERTS)]
    denom = wsel[0]
    for e in range(1, N_EXPERTS):
        denom = denom + wsel[e]
    selm = jnp.concatenate([jnp.where(chosen[e], 1.0, 0.0) for e in range(N_EXPERTS)], axis=0)
    ii = lax.broadcasted_iota(I32, (tm, tm), 0)
    jj = lax.broadcasted_iota(I32, (tm, tm), 1)
    tri = jnp.where(ii < jj, 1.0, 0.0).astype(BF16)
    prefix = _dot(selm.astype(BF16), tri)
    cnt = jnp.sum(selm, axis=1, keepdims=True)
    cnt_ref[...] = jnp.broadcast_to(cnt, cnt_ref.shape).astype(I32)
    seen = jnp.zeros(best.shape, I32)
    e0 = jnp.full(best.shape, -1, I32)
    e1 = jnp.full(best.shape, -1, I32)
    r0 = jnp.full(best.shape, -4 * RCOMP, I32)
    r1 = jnp.full(best.shape, -4 * RCOMP, I32)
    g0 = jnp.zeros(best.shape, F32)
    g1_ = jnp.zeros(best.shape, F32)
    for e in range(N_EXPERTS):
        first = chosen[e] & (seen == 0)
        second = chosen[e] & (seen == 1)
        pe = prefix[e:e + 1, :].astype(I32)
        ge = wsel[e] / denom
        e0 = jnp.where(first, e, e0)
        e1 = jnp.where(second, e, e1)
        r0 = jnp.where(first, pe, r0)
        r1 = jnp.where(second, pe, r1)
        g0 = jnp.where(first, ge, g0)
        g1_ = jnp.where(second, ge, g1_)
        seen = seen + jnp.where(chosen[e], 1, 0)
    ri_ref[...] = jnp.zeros(ri_ref.shape, I32)
    rf_ref[...] = jnp.zeros(rf_ref.shape, F32)
    ri_ref[0:1, :] = e0
    ri_ref[1:2, :] = e1
    ri_ref[2:3, :] = r0
    ri_ref[3:4, :] = r1
    rf_ref[0:1, :] = g0
    rf_ref[1:2, :] = g1_


def _tail_specs(n, d, tm, mod, layer, tpb):
    assert tm == TMD
    one = pl.BlockSpec((1, d), lambda t: (0, 0))
    in_specs = [pl.BlockSpec((tm, d), lambda t: (t, 0)),
                mod.vec(layer, 2, tpb), one, mod.vec(layer, 3, tpb), mod.vec(layer, 4, tpb),
                pl.BlockSpec((N_EXPERTS, d), lambda t: (0, 0)),
                pl.BlockSpec((N_EXPERTS, 1), lambda t: (0, 0))]
    out_specs = [pl.BlockSpec((tm, d), lambda t: (t, 0)),
                 pl.BlockSpec((tm, d), lambda t: (t, 0)),
                 pl.BlockSpec((SUBLANES_F32, tm), lambda t: (0, t)),
                 pl.BlockSpec((SUBLANES_F32, tm), lambda t: (0, t)),
                 pl.BlockSpec((None, N_EXPERTS, LANES), lambda t: (t, 0, 0))]
    out_shape = [jax.ShapeDtypeStruct((n, d), F32), jax.ShapeDtypeStruct((n, d), BF16),
                 jax.ShapeDtypeStruct((SUBLANES_F32, n), I32), jax.ShapeDtypeStruct((SUBLANES_F32, n), F32),
                 jax.ShapeDtypeStruct((n // tm, N_EXPERTS, LANES), I32)]
    return in_specs, out_specs, out_shape


def _outproj_kernel(tpb, a_ref, gb_ref, u_ref, up_ref, un_ref, cw_ref, wa_ref, wb_ref,
                    x_ref, g1_ref, n2g_ref, sh2_ref, sc2_ref, rwt_ref, rb_ref,
                    xo_ref, ho_ref, ri_ref, rf_ref, cnt_ref):
    t = pl.program_id(0)
    tm = u_ref.shape[0]
    u = u_ref[...].astype(F32)
    keep_prev = jnp.where(t % tpb == 0, 0.0, 1.0)
    keep_next = jnp.where(t % tpb == tpb - 1, 0.0, 1.0)
    prev_row = up_ref[HALO - 1:HALO, :].astype(F32) * keep_prev
    next_row = un_ref[0:1, :].astype(F32) * keep_next
    row = lax.broadcasted_iota(I32, u.shape, 0)
    u_m1 = jnp.where(row == 0, prev_row, pltpu.roll(u, 1, 0))
    u_p1 = jnp.where(row == tm - 1, next_row, pltpu.roll(u, tm - 1, 0))
    conv = cw_ref[0:1, :] * u_m1 + cw_ref[1:2, :] * u + cw_ref[2:3, :] * u_p1
    bx = (gb_ref[...].astype(F32) * conv).astype(BF16)
    y = _dot(a_ref[...], wa_ref[...]) + _dot(bx, wb_ref[...])
    _residual_norm_route(x_ref[...], y, g1_ref[...], n2g_ref[...], sh2_ref[...], sc2_ref[...],
                         rwt_ref[...], rb_ref[...], xo_ref, ho_ref, ri_ref, rf_ref, cnt_ref)


def _outproj(a, gb, u, cw, wa, wb, x2, mod, n2g, rwt, rb, tpb):
    n, d = x2.shape
    tm = TMD
    w = NA_WIDTH
    hb = tm // HALO
    nh = n // HALO
    half = pl.BlockSpec((tm, w), lambda t: (t, 0))
    tail_in, out_specs, out_shape = _tail_specs(n, d, tm, mod, 0, tpb)
    in_specs = [half, half, half,
                pl.BlockSpec((HALO, w), lambda t: (jnp.maximum(t * hb - 1, 0), 0)),
                pl.BlockSpec((HALO, w), lambda t: (jnp.minimum((t + 1) * hb, nh - 1), 0)),
                pl.BlockSpec(cw.shape, lambda t: (0, 0)),
                pl.BlockSpec(wa.shape, lambda t: (0, 0)),
                pl.BlockSpec(wb.shape, lambda t: (0, 0))] + tail_in
    return pl.pallas_call(
        functools.partial(_outproj_kernel, tpb),
        grid=(n // tm,),
        in_specs=in_specs, out_specs=out_specs, out_shape=out_shape,
        compiler_params=_cparams(("arbitrary",)),
        name="outproj_route",
    )(a, gb, u, u, u, cw, wa, wb, x2, mod.arr, n2g, mod.arr, mod.arr, rwt, rb)


def _conf_kernel(tpb, u_ref, up_ref, un_ref, dww_ref, dwb_ref, lng_ref, lnb_ref, w2_ref, b2_ref,
                 x_ref, g1_ref, n2g_ref, sh2_ref, sc2_ref, rwt_ref, rb_ref,
                 xo_ref, ho_ref, ri_ref, rf_ref, cnt_ref, ue_ref, sh_ref, conv_ref, wb_ref):
    t = pl.program_id(0)
    tm, d = u_ref.shape
    taps = dww_ref.shape[0]
    sl = SUBLANES_F32

    @pl.when(t == 0)
    def _():
        for k in range(taps):
            wb_ref[k * sl:(k + 1) * sl, :] = jnp.broadcast_to(dww_ref[k:k + 1, :], (sl, d))

    keep_prev = jnp.where(t % tpb == 0, 0.0, 1.0)
    keep_next = jnp.where(t % tpb == tpb - 1, 0.0, 1.0)
    ue_ref[0:HALO, :] = up_ref[...].astype(F32) * keep_prev
    ue_ref[HALO:HALO + tm, :] = u_ref[...].astype(F32)
    ue_ref[HALO + tm:HALO + tm + HALO, :] = un_ref[...].astype(F32) * keep_next
    ext = tm + 2 * HALO - sl
    for s in range(1, sl):
        sh_ref[s - 1, :, :] = ue_ref[s:s + ext, :]
    off0 = HALO - (taps - 1) // 2
    nv = CONV_RC // sl
    for cb in range(d // LANES):
        cols = slice(cb * LANES, (cb + 1) * LANES)

        def body(rc, carry, cols=cols):
            r0 = pl.multiple_of(rc * CONV_RC, CONV_RC)
            accs = [None] * nv
            for k in range(taps):
                s, a = (k + off0) % sl, (k + off0) // sl
                wk = wb_ref[k * sl:(k + 1) * sl, cols]
                for j in range(nv):
                    rows = pl.ds(r0 + sl * (a + j), sl)
                    win = ue_ref[rows, cols] if s == 0 else sh_ref[s - 1, rows, cols]
                    term = wk * win
                    accs[j] = term if accs[j] is None else accs[j] + term
            for j in range(nv):
                conv_ref[pl.ds(r0 + sl * j, sl), cols] = accs[j]
            return carry

        lax.fori_loop(0, tm // CONV_RC, body, 0)
    c = conv_ref[...] + dwb_ref[...]
    mu = jnp.mean(c, axis=-1, keepdims=True)
    cc = c - mu
    var = jnp.mean(cc * cc, axis=-1, keepdims=True)
    z = cc * lax.rsqrt(var + EPS) * lng_ref[...] + lnb_ref[...]
    z = (z * _sigmoid(z)).astype(BF16)
    y = _dot(z, w2_ref[...]) + b2_ref[...]
    _residual_norm_route(x_ref[...], y, g1_ref[...], n2g_ref[...], sh2_ref[...], sc2_ref[...],
                         rwt_ref[...], rb_ref[...], xo_ref, ho_ref, ri_ref, rf_ref, cnt_ref)


def _conf(u, dww, dwb, lng, lnb, w2, b2, x2, mod, n2g, rwt, rb, tpb):
    n, d = x2.shape
    tm = TMD
    hb = tm // HALO
    nh = n // HALO
    taps = dww.shape[0]
    assert (taps - 1) // 2 <= HALO - 1 and taps // 2 <= HALO
    one = pl.BlockSpec((1, d), lambda t: (0, 0))
    tail_in, out_specs, out_shape = _tail_specs(n, d, tm, mod, 1, tpb)
    in_specs = [pl.BlockSpec((tm, d), lambda t: (t, 0)),
                pl.BlockSpec((HALO, d), lambda t: (jnp.maximum(t * hb - 1, 0), 0)),
                pl.BlockSpec((HALO, d), lambda t: (jnp.minimum((t + 1) * hb, nh - 1), 0)),
                pl.BlockSpec(dww.shape, lambda t: (0, 0)), one, one, one,
                pl.BlockSpec(w2.shape, lambda t: (0, 0)), one] + tail_in
    ext = tm + 2 * HALO - SUBLANES_F32
    return pl.pallas_call(
        functools.partial(_conf_kernel, tpb),
        grid=(n // tm,),
        in_specs=in_specs, out_specs=out_specs, out_shape=out_shape,
        scratch_shapes=[pltpu.VMEM((tm + 2 * HALO, d), F32),
                        pltpu.VMEM((SUBLANES_F32 - 1, ext, d), F32),
                        pltpu.VMEM((tm, d), F32),
                        pltpu.VMEM((taps * SUBLANES_F32, d), F32)],
        compiler_params=_cparams(("arbitrary",)),
        name="conf_conv_route",
    )(u, u, u, dww, dwb, lng, lnb, w2, b2, x2, mod.arr, n2g, mod.arr, mod.arr, rwt, rb)


def _sorted_rows(n):
    t_n = n // TMD
    rows = 2 * n + (CHUNK - 1) * t_n * N_EXPERTS + N_EXPERTS * (TME - CHUNK)
    return (rows + TME - 1) // TME * TME


def _moe_plan(cnt, n):
    t_n, e_n = cnt.shape
    rt = _sorted_rows(n)
    seg = (cnt + CHUNK - 1) // CHUNK * CHUNK
    lo = jnp.cumsum(seg, axis=1) - seg
    etot = jnp.sum(seg, axis=0)
    epad = (etot + TME - 1) // TME * TME
    eend = jnp.cumsum(epad)
    estart = eend - epad
    go = estart[None, :] + jnp.cumsum(seg, axis=0) - seg
    nchunk = jnp.sum(seg, axis=1) // CHUNK
    c_rows = jnp.arange(CMAX, dtype=I32) * CHUNK
    e_ids = jnp.arange(e_n, dtype=I32)
    owner = jnp.sum(((lo + seg)[:, None, :] <= c_rows[None, :, None]).astype(I32), axis=-1)
    owner = jnp.minimum(owner, e_n - 1)
    shift = jnp.sum(jnp.where(owner[:, :, None] == e_ids, (go - lo)[:, None, :], 0), axis=-1)
    cmap = jnp.clip((shift + c_rows[None, :]) // CHUNK, 0, rt // CHUNK - 1)
    npad = (epad - etot) // CHUNK
    pend = jnp.cumsum(npad)
    kk = jnp.arange(e_n * (TME // CHUNK - 1), dtype=I32)
    pown = jnp.minimum(jnp.sum((pend[None, :] <= kk[:, None]).astype(I32), axis=-1), e_n - 1)
    pbase = (estart + etot) // CHUNK - (pend - npad)
    padmap = kk + jnp.sum(jnp.where(pown[:, None] == e_ids, pbase[None, :], 0), axis=-1)
    padmap = jnp.clip(padmap, 0, rt // CHUNK - 1)
    tile_row = jnp.arange(rt // TME, dtype=I32) * TME
    texp = jnp.minimum(jnp.sum((eend[None, :] <= tile_row[:, None]).astype(I32), axis=-1), e_n - 1)
    as_i32 = lambda a: a.astype(I32)
    return dict(lo=as_i32(lo.reshape(-1)), nchunk=as_i32(nchunk), cmap=as_i32(cmap.reshape(-1)),
                padmap=as_i32(padmap), npadtot=as_i32(pend[-1:]), texp=as_i32(texp),
                nact=as_i32(eend[-1:] // TME))


def _chunk_copy(src_ref, dst_ref, sem):
    return pltpu.make_async_copy(src_ref, dst_ref, sem)


def _wait_chunks(n_chunks, desc):
    for bit in range(CMAX.bit_length()):
        @pl.when(((n_chunks >> bit) & 1) == 1)
        def _(bit=bit):
            desc(CHUNK << bit).wait()


def _dispatch_kernel(nchunk_ref, cmap_ref, npad_ref, padmap_ref, nact_ref, lo_ref,
                     h_ref, ri_ref, xs_ref, lp_ref, xc_ref, z_ref, sem, zsem):
    t = pl.program_id(0)
    nt = pl.num_programs(0)
    slot = t % 2
    ntail = xs_ref.shape[0] // TME - nact_ref[0]

    def pad_copy(k):
        g = pl.multiple_of(padmap_ref[k] * CHUNK, CHUNK)
        return _chunk_copy(z_ref.at[0:CHUNK, :], xs_ref.at[pl.ds(g, CHUNK), :], zsem.at[0])

    def tail_copy(k):
        g = pl.multiple_of((nact_ref[0] + k) * TME, TME)
        return _chunk_copy(z_ref, xs_ref.at[pl.ds(g, TME), :], zsem.at[0])

    @pl.when(t == 0)
    def _():
        z_ref[...] = jnp.zeros(z_ref.shape, z_ref.dtype)
        lax.fori_loop(0, npad_ref[0], lambda k, c: (pad_copy(k).start(), c)[1], 0)
        lax.fori_loop(0, ntail, lambda k, c: (tail_copy(k).start(), c)[1], 0)

    e0, e1 = ri_ref[0:1, :], ri_ref[1:2, :]
    b0 = jnp.zeros(e0.shape, I32)
    b1 = jnp.zeros(e0.shape, I32)
    for e in range(N_EXPERTS):
        lo_e = lo_ref[t * N_EXPERTS + e]
        b0 = jnp.where(e0 == e, lo_e, b0)
        b1 = jnp.where(e1 == e, lo_e, b1)
    lp0 = b0 + ri_ref[2:3, :]
    lp1 = b1 + ri_ref[3:4, :]
    lp_ref[...] = jnp.zeros(lp_ref.shape, I32)
    lp_ref[0:1, :] = lp0
    lp_ref[1:2, :] = lp1

    rows = lax.broadcasted_iota(I32, (RCOMP, TMD), 0)
    onehot = jnp.where((lp0 == rows) | (lp1 == rows), 1.0, 0.0).astype(BF16)
    xc_ref[slot] = _dot(onehot, h_ref[...]).astype(BF16)

    def out_copy(tt, sl, c):
        g = pl.multiple_of(cmap_ref[tt * CMAX + c] * CHUNK, CHUNK)
        l = pl.multiple_of(c * CHUNK, CHUNK)
        return _chunk_copy(xc_ref.at[sl, pl.ds(l, CHUNK), :], xs_ref.at[pl.ds(g, CHUNK), :], sem.at[sl])

    lax.fori_loop(0, nchunk_ref[t], lambda c, cr: (out_copy(t, slot, c).start(), cr)[1], 0)

    def out_desc(sl):
        return lambda rows: _chunk_copy(xc_ref.at[sl, 0:rows, :], xs_ref.at[0:rows, :], sem.at[sl])

    @pl.when(t > 0)
    def _():
        _wait_chunks(nchunk_ref[t - 1], out_desc(1 - slot))

    @pl.when(t == 0)
    def _():
        lax.fori_loop(0, npad_ref[0], lambda k, c: (pad_copy(k).wait(), c)[1], 0)
        lax.fori_loop(0, ntail, lambda k, c: (tail_copy(k).wait(), c)[1], 0)

    @pl.when(t == nt - 1)
    def _():
        _wait_chunks(nchunk_ref[t], out_desc(slot))


def _dispatch(h2, ri, plan):
    n, d = h2.shape
    t_n = n // TMD
    rt = _sorted_rows(n)
    grid_spec = pltpu.PrefetchScalarGridSpec(
        num_scalar_prefetch=6,
        grid=(t_n,),
        in_specs=[pl.BlockSpec((TMD, d), lambda t, *_: (t, 0)),
                  pl.BlockSpec((SUBLANES_F32, TMD), lambda t, *_: (0, t))],
        out_specs=[pl.BlockSpec(memory_space=pl.ANY),
                   pl.BlockSpec((None, SUBLANES_F32, TMD), lambda t, *_: (t, 0, 0))],
        scratch_shapes=[pltpu.VMEM((2, RCOMP, d), BF16),
                        pltpu.VMEM((TME, d), BF16),
                        pltpu.SemaphoreType.DMA((2,)),
                        pltpu.SemaphoreType.DMA((1,))],
    )
    return pl.pallas_call(
        _dispatch_kernel,
        grid_spec=grid_spec,
        out_shape=[jax.ShapeDtypeStruct((rt, d), BF16),
                   jax.ShapeDtypeStruct((t_n, SUBLANES_F32, TMD), I32)],
        compiler_params=_cparams(("arbitrary",)),
        name="moe_dispatch",
    )(plan["nchunk"], plan["cmap"], plan["npadtot"], plan["padmap"], plan["nact"], plan["lo"], h2, ri)


def _expert_kernel(texp_ref, nact_ref, x_ref, w1_ref, w3_ref, w2_ref, y_ref, w1b, w3b, w2b):
    i = pl.program_id(0)

    @pl.when(i < nact_ref[0])
    def _():
        prev = texp_ref[jnp.maximum(i - 1, 0)]

        @pl.when((i == 0) | (texp_ref[i] != prev))
        def _():
            w1b[...] = w1_ref[...].astype(BF16)
            w3b[...] = w3_ref[...].astype(BF16)
            w2b[...] = w2_ref[...].astype(BF16)

        x = x_ref[...]
        h = _dot(x, w1b[...])
        g = _dot(x, w3b[...])
        a = (h * _sigmoid(h) * g).astype(BF16)
        y_ref[...] = _dot(a, w2b[...]).astype(y_ref.dtype)


def _experts(xs, w1, w3, w2, layer, plan):
    rt, d = xs.shape
    f = w1.shape[-1]
    nt = rt // TME

    def row_map(i, texp, nact):
        return (jnp.clip(i, 0, jnp.maximum(nact[0] - 1, 0)), 0)

    def w_map(i, texp, nact):
        return (layer, texp[jnp.clip(i, 0, jnp.maximum(nact[0] - 1, 0))], 0, 0)

    grid_spec = pltpu.PrefetchScalarGridSpec(
        num_scalar_prefetch=2,
        grid=(nt,),
        in_specs=[pl.BlockSpec((TME, d), row_map),
                  pl.BlockSpec((None, None, d, f), w_map),
                  pl.BlockSpec((None, None, d, f), w_map),
                  pl.BlockSpec((None, None, f, d), w_map)],
        out_specs=pl.BlockSpec((TME, d), row_map),
        scratch_shapes=[pltpu.VMEM((d, f), BF16), pltpu.VMEM((d, f), BF16), pltpu.VMEM((f, d), BF16)],
    )
    return pl.pallas_call(
        _expert_kernel,
        grid_spec=grid_spec,
        out_shape=jax.ShapeDtypeStruct((rt, d), BF16),
        input_output_aliases={2: 0},
        compiler_params=_cparams(("arbitrary",)),
        name="moe_experts",
    )(plan["texp"], plan["nact"], xs, w1, w3, w2)


def _combine_core(nchunk_ref, cmap_ref, ys_ref, ci_ref, cf_ref, x_ref, g2_ref, yc_ref, sem):
    t = pl.program_id(0)
    nt = pl.num_programs(0)
    slot = t % 2

    def in_copy(tt, sl, c):
        g = pl.multiple_of(cmap_ref[tt * CMAX + c] * CHUNK, CHUNK)
        l = pl.multiple_of(c * CHUNK, CHUNK)
        return _chunk_copy(ys_ref.at[pl.ds(g, CHUNK), :], yc_ref.at[sl, pl.ds(l, CHUNK), :], sem.at[sl])

    @pl.when(t == 0)
    def _():
        yc_ref[...] = jnp.zeros(yc_ref.shape, yc_ref.dtype)
        lax.fori_loop(0, nchunk_ref[0], lambda c, cr: (in_copy(0, 0, c).start(), cr)[1], 0)

    @pl.when(t + 1 < nt)
    def _():
        lax.fori_loop(0, nchunk_ref[t + 1], lambda c, cr: (in_copy(t + 1, 1 - slot, c).start(), cr)[1], 0)

    _wait_chunks(nchunk_ref[t],
                 lambda rows: _chunk_copy(ys_ref.at[0:rows, :], yc_ref.at[slot, 0:rows, :], sem.at[slot]))

    ci = ci_ref[...]
    cf = cf_ref[...]
    cols = lax.broadcasted_iota(I32, (TMD, RCOMP), 1)
    w = jnp.where(ci[:, 0:1] == cols, cf[:, 0:1], 0.0) + jnp.where(ci[:, 1:2] == cols, cf[:, 1:2], 0.0)
    moe = _dot(w.astype(BF16), yc_ref[slot])
    return x_ref[...] + g2_ref[...] * moe


def _combine_final_kernel(nchunk_ref, cmap_ref, ys_ref, ci_ref, cf_ref, x_ref, g2_ref, fg_ref,
                          o_ref, yc_ref, sem):
    x1 = _combine_core(nchunk_ref, cmap_ref, ys_ref, ci_ref, cf_ref, x_ref, g2_ref, yc_ref, sem)
    ms = jnp.mean(x1 * x1, axis=-1, keepdims=True)
    o_ref[...] = x1 * lax.rsqrt(ms + EPS) * fg_ref[...]


def _combine_pw1_kernel(nchunk_ref, cmap_ref, ys_ref, ci_ref, cf_ref, x_ref, g2_ref,
                        n1g_ref, sh_ref, sc_ref, w_ref, b_ref, o_ref, u_ref, yc_ref, sem):
    x1 = _combine_core(nchunk_ref, cmap_ref, ys_ref, ci_ref, cf_ref, x_ref, g2_ref, yc_ref, sem)
    o_ref[...] = x1
    d = x1.shape[1]
    h = _norm_mod(x1, n1g_ref[...], sh_ref[...], sc_ref[...]).astype(BF16)
    a = _dot(h, w_ref[:, 0:d]) + b_ref[:, 0:d]
    g = _dot(h, w_ref[:, d:2 * d]) + b_ref[:, d:2 * d]
    u_ref[...] = (a * _sigmoid(g)).astype(BF16)


def _combine(ys, plan, lp, rf, x2, mod, layer, tpb, tail):
    n, d = x2.shape
    t_n = n // TMD
    col = pl.BlockSpec((TMD, SUBLANES_F32), lambda t, *_: (t, 0))
    one = pl.BlockSpec((1, d), lambda t, *_: (0, 0))
    tile = pl.BlockSpec((TMD, d), lambda t, *_: (t, 0))
    in_specs = [pl.BlockSpec(memory_space=pl.ANY), col, col, tile, mod.vec(layer, 5, tpb)]
    args = [plan["nchunk"], plan["cmap"], ys,
            jnp.transpose(lp, (0, 2, 1)).reshape(n, SUBLANES_F32), rf.T, x2, mod.arr]
    if tail[0] == "final":
        body, name = _combine_final_kernel, "moe_combine_final"
        in_specs += [one]
        args += [tail[1]]
        out_specs = tile
        out_shape = jax.ShapeDtypeStruct((n, d), F32)
    else:
        body, name = _combine_pw1_kernel, "moe_combine_pw1"
        _, n1g, w_bf, bias = tail
        in_specs += [one, mod.vec(layer + 1, 0, tpb), mod.vec(layer + 1, 1, tpb),
                     pl.BlockSpec(w_bf.shape, lambda t, *_: (0, 0)),
                     pl.BlockSpec(bias.shape, lambda t, *_: (0, 0))]
        args += [n1g, mod.arr, mod.arr, w_bf, bias]
        out_specs = [tile, tile]
        out_shape = [jax.ShapeDtypeStruct((n, d), F32), jax.ShapeDtypeStruct((n, d), BF16)]
    grid_spec = pltpu.PrefetchScalarGridSpec(
        num_scalar_prefetch=2,
        grid=(t_n,),
        in_specs=in_specs,
        out_specs=out_specs,
        scratch_shapes=[pltpu.VMEM((2, RCOMP, d), BF16), pltpu.SemaphoreType.DMA((2,))],
    )
    return pl.pallas_call(body, grid_spec=grid_spec, out_shape=out_shape,
                          compiler_params=_cparams(("arbitrary",)), name=name)(*args)


def _moe(x1, h2, ri, rf, cnt, w1, w3, w2, layer, mod, tpb, tail):
    n = x1.shape[0]
    plan = _moe_plan(cnt[:, :, 0], n)
    xs, lp = _dispatch(h2, ri, plan)
    ys = _experts(xs, w1, w3, w2, layer, plan)
    return _combine(ys, plan, lp, rf, x1, mod, layer, tpb, tail)


def kernel(x, c, ctx, c_ctx, norm1_g, norm2_g, w_mod, b_mod, na_w_in, na_rpb, sc_conv_w, ab_w_out,
           cf_pw1_w, cf_pw1_b, cf_dw_w, cf_dw_b, cf_ln_g, cf_ln_b, cf_pw2_w, cf_pw2_b,
           router_w, router_b, moe_w1, moe_w3, moe_w2, final_g):
    b, s, d = x.shape
    l = ctx.shape[1]
    n = b * s
    depth = w_mod.shape[0]
    assert depth == 2, "layer pattern (attention layer, Conformer layer) is written out for depth 2"
    assert s % TM_PROJ == 0 and s % TMD == 0
    rows = s // GRID_W
    assert rows % ATT_ROWS == 0 and rows >= ATT_WIN and s % ATT_KBLK == 0
    tpb = s // TM_PROJ
    tpb_d = s // TMD

    mr = (b + 1 + SUBLANES_F32 - 1) // SUBLANES_F32 * SUBLANES_F32
    c_all = jnp.concatenate([c, c_ctx[None, :], jnp.zeros((mr - b - 1, d), F32)], axis=0)
    mod = _Mod(_modulation(c_all, w_mod, b_mod), d)

    x2 = x.reshape(n, d)
    rwt = router_w.T
    rb = router_b.reshape(N_EXPERTS, 1)
    fg = final_g.reshape(1, d)

    w_in = na_w_in[0].astype(BF16)
    n1g = norm1_g[0].reshape(1, d)
    wkt = w_in[:, NA_WIDTH:2 * NA_WIDTH].T
    q, kt, v, gb, u = _inproj(x2, n1g, mod, w_in, wkt, tpb)
    kct, vc = _ctxproj(ctx.reshape(b * l, d), n1g, mod, b, wkt, w_in[:, 2 * NA_WIDTH:3 * NA_WIDTH], l)
    tab = _attn_bias_tables(na_rpb[0], rows)
    a = _attention(q, kt, v, kct, vc, tab, b, s, l)
    w_out = ab_w_out[0].astype(BF16)
    x1, h2, ri, rf, cnt = _outproj(a, gb, u, sc_conv_w[0], w_out[:NA_WIDTH], w_out[NA_WIDTH:], x2,
                                   mod, norm2_g[0].reshape(1, d), rwt, rb, tpb_d)
    x2, u = _moe(x1, h2, ri, rf, cnt, moe_w1, moe_w3, moe_w2, 0, mod, tpb_d,
                 ("pw1", norm1_g[1].reshape(1, d), cf_pw1_w[0].astype(BF16), cf_pw1_b[0].reshape(1, 2 * d)))
    x1, h2, ri, rf, cnt = _conf(u, cf_dw_w[0], cf_dw_b[0].reshape(1, d), cf_ln_g[0].reshape(1, d),
                                cf_ln_b[0].reshape(1, d), cf_pw2_w[0].astype(BF16),
                                cf_pw2_b[0].reshape(1, d), x2, mod, norm2_g[1].reshape(1, d),
                                rwt, rb, tpb_d)
    out = _moe(x1, h2, ri, rf, cnt, moe_w1, moe_w3, moe_w2, 1, mod, tpb_d, ("final", fg))
    return out.reshape(b, s, d)
```

```python
import functools

import numpy as np
import jax
import jax.numpy as jnp
from jax import lax
from jax.experimental import pallas as pl
from jax.experimental.pallas import tpu as pltpu

F32 = jnp.float32
BF16 = jnp.bfloat16
I32 = jnp.int32

GRID_W = 64
NA_HEADS = 8
NA_HEAD_DIM = 64
NA_WIDTH = NA_HEADS * NA_HEAD_DIM
NA_KR = 8
NA_KC = 16
N_EXPERTS = 16
N_GROUPS = 4
EXPERTS_PER_GROUP = N_EXPERTS // N_GROUPS
N_MOD = 6
EPS = 1e-6
NEG = -1e30
LOG2E = 1.4426950408889634

LANES = 128
SUBLANES_F32 = 8
SUBLANES_BF16 = 16

TM_PROJ = 1024
ATT_ROWS = 8
ATT_WIN = 2 * ATT_ROWS
ATT_KBLK = 256
ATT_HEADS = 4
TMD = 512
CHUNK = SUBLANES_BF16
CMAX = (2 * TMD + (N_EXPERTS - 1) * CHUNK + CHUNK - 1) // CHUNK + 1
RCOMP = CMAX * CHUNK
PAIRS_MAX = CMAX // 2
COPY_LIST = 2 * PAIRS_MAX + 2 * N_EXPERTS
TME = 1024
CONV_RC = 128
HALO = SUBLANES_BF16
VMEM_LIMIT = 52 * 1024 * 1024


def _cparams(sem):
    return pltpu.CompilerParams(dimension_semantics=sem, vmem_limit_bytes=VMEM_LIMIT)


def _dot(a, b):
    return jnp.dot(a, b, preferred_element_type=F32)


def _dot_nt(a, b, precision=None):
    return lax.dot_general(a, b, (((1,), (1,)), ((), ())), precision=precision,
                           preferred_element_type=F32)


def _sigmoid(x):
    return 1.0 / (1.0 + jnp.exp(-x))


def _norm_mod(x, g, sh, sc):
    ms = jnp.mean(x * x, axis=-1, keepdims=True)
    y = x * lax.rsqrt(ms + EPS) * g
    return y * (1.0 + sc) + sh


def _split_bf16(a):
    hi = a.astype(BF16)
    lo = (a - hi.astype(F32)).astype(BF16)
    return hi, lo


def _mod_kernel(c_ref, w_ref, b_ref, o_ref):
    c = c_ref[...]
    mr = c.shape[0]
    hi, lo = _split_bf16(c * _sigmoid(c))
    w_hi, w_lo = _split_bf16(w_ref[...])
    r = _dot(jnp.concatenate([hi, lo], axis=0), w_hi)
    o_ref[...] = r[:mr] + r[mr:] + _dot(hi, w_lo) + b_ref[...]


def _modulation(c_all, w_mod, b_mod):
    depth, d, n6 = w_mod.shape
    mr = c_all.shape[0]
    tn = 1024
    return pl.pallas_call(
        _mod_kernel,
        grid=(depth, n6 // tn),
        in_specs=[pl.BlockSpec((mr, d), lambda i, j: (0, 0)),
                  pl.BlockSpec((None, d, tn), lambda i, j: (i, 0, j)),
                  pl.BlockSpec((None, 1, tn), lambda i, j: (i, 0, j))],
        out_specs=pl.BlockSpec((None, mr, tn), lambda i, j: (i, 0, j)),
        out_shape=jax.ShapeDtypeStruct((depth, mr, n6), F32),
        compiler_params=_cparams(("arbitrary", "arbitrary")),
        name="modulation",
    )(c_all, w_mod, b_mod.reshape(depth, 1, n6))


class _Mod:
    def __init__(self, mod, d):
        self.depth, self.mr, _ = mod.shape
        self.d = d
        self.arr = mod.reshape(self.depth * self.mr * N_MOD, 1, d)

    def vec(self, layer, j, tpb):
        mr = self.mr
        return pl.BlockSpec((None, 1, self.d),
                            lambda t, *_: ((layer * mr + t // tpb) * N_MOD + j, 0, 0))

    def row(self, layer, j, r):
        mr = self.mr
        return pl.BlockSpec((None, 1, self.d), lambda t, *_: ((layer * mr + r) * N_MOD + j, 0, 0))


def _inproj_kernel(x_ref, g_ref, sh_ref, sc_ref, w_ref, wkt_ref, q_ref, kt_ref, v_ref, gb_ref, u_ref):
    h = _norm_mod(x_ref[...], g_ref[...], sh_ref[...], sc_ref[...]).astype(BF16)
    w = NA_WIDTH
    q_ref[...] = (_dot(h, w_ref[:, 0:w]) * (NA_HEAD_DIM ** -0.5 * LOG2E)).astype(BF16)
    kt_ref[...] = _dot_nt(wkt_ref[...], h).astype(BF16)
    v_ref[...] = _dot(h, w_ref[:, 2 * w:3 * w]).astype(BF16)
    gb_ref[...] = _dot(h, w_ref[:, 3 * w:4 * w]).astype(BF16)
    u_ref[...] = (_dot(h, w_ref[:, 4 * w:5 * w]) * _dot(h, w_ref[:, 5 * w:6 * w])).astype(BF16)


def _inproj(x2, g, mod, w_bf, wkt_bf, tpb):
    n, d = x2.shape
    tm = TM_PROJ
    out = jax.ShapeDtypeStruct((n, NA_WIDTH), BF16)
    ospec = pl.BlockSpec((tm, NA_WIDTH), lambda t: (t, 0))
    out_t = jax.ShapeDtypeStruct((NA_WIDTH, n), BF16)
    ospec_t = pl.BlockSpec((NA_WIDTH, tm), lambda t: (0, t))
    return pl.pallas_call(
        _inproj_kernel,
        grid=(n // tm,),
        in_specs=[pl.BlockSpec((tm, d), lambda t: (t, 0)),
                  pl.BlockSpec((1, d), lambda t: (0, 0)), mod.vec(0, 0, tpb), mod.vec(0, 1, tpb),
                  pl.BlockSpec(w_bf.shape, lambda t: (0, 0)),
                  pl.BlockSpec(wkt_bf.shape, lambda t: (0, 0))],
        out_specs=[ospec, ospec_t, ospec, ospec, ospec],
        out_shape=[out, out_t, out, out, out],
        compiler_params=_cparams(("arbitrary",)),
        name="inproj",
    )(x2, g, mod.arr, mod.arr, w_bf, wkt_bf)


def _ctxproj_kernel(x_ref, g_ref, sh_ref, sc_ref, wkt_ref, wv_ref, kt_ref, v_ref):
    h = _norm_mod(x_ref[...], g_ref[...], sh_ref[...], sc_ref[...]).astype(BF16)
    kt_ref[...] = _dot_nt(wkt_ref[...], h).astype(BF16)
    v_ref[...] = _dot(h, wv_ref[...]).astype(BF16)


def _ctxproj(ctx2, g, mod, ctx_row, wkt_bf, wv_bf, l):
    n, d = ctx2.shape
    return pl.pallas_call(
        _ctxproj_kernel,
        grid=(n // l,),
        in_specs=[pl.BlockSpec((l, d), lambda t: (t, 0)), pl.BlockSpec((1, d), lambda t: (0, 0)),
                  mod.row(0, 0, ctx_row), mod.row(0, 1, ctx_row),
                  pl.BlockSpec(wkt_bf.shape, lambda t: (0, 0)),
                  pl.BlockSpec(wv_bf.shape, lambda t: (0, 0))],
        out_specs=[pl.BlockSpec((NA_WIDTH, l), lambda t: (0, t)),
                   pl.BlockSpec((l, NA_WIDTH), lambda t: (t, 0))],
        out_shape=[jax.ShapeDtypeStruct((NA_WIDTH, n), BF16), jax.ShapeDtypeStruct((n, NA_WIDTH), BF16)],
        compiler_params=_cparams(("arbitrary",)),
        name="ctxproj",
    )(ctx2, g, mod.arr, mod.arr, wkt_bf, wv_bf)


def _bias_row_index(rows):
    nb = rows // ATT_ROWS
    n_dr = 2 * NA_KR - 1
    dr_idx = np.full((3, ATT_ROWS, ATT_WIN), n_dr, np.int32)
    for ty, j in enumerate((0, 1, nb - 1)):
        ws = ATT_ROWS * j - NA_KR // 2
        for qr in range(ATT_ROWS):
            r = ATT_ROWS * j + qr
            rs = int(np.clip(r - NA_KR // 2, 0, rows - NA_KR))
            for kr in range(ATT_WIN):
                key_row = ws + kr
                if rs <= key_row < rs + NA_KR:
                    dr_idx[ty, qr, kr] = key_row - r + (NA_KR - 1)
    kr_blk = ATT_KBLK // GRID_W
    assert (dr_idx[:, :ATT_ROWS // 2, ATT_WIN - kr_blk:] == n_dr).all()
    assert (dr_idx[:, ATT_ROWS // 2:, :kr_blk] == n_dr).all()
    return dr_idx


def _bias_kernel(dr_idx, rpb_ref, out_ref, t_ref):
    h = pl.program_id(0)
    n_dr = 2 * NA_KR - 1
    n_dc = 2 * NA_KC - 1
    qc = lax.broadcasted_iota(I32, (GRID_W, GRID_W), 0)
    kc = lax.broadcasted_iota(I32, (GRID_W, GRID_W), 1)
    cs = jnp.clip(qc - NA_KC // 2, 0, GRID_W - NA_KC)
    valid = (kc >= cs) & (kc < cs + NA_KC)
    dc = jnp.where(valid, kc - qc + (NA_KC - 1), -1)
    for dr in range(n_dr):
        slab = jnp.full((GRID_W, GRID_W), NEG, F32)
        for m in range(n_dc):
            slab = jnp.where(dc == m, rpb_ref[(h * n_dr + dr) * n_dc + m] * LOG2E, slab)
        t_ref[dr] = slab
    t_ref[n_dr] = jnp.full((GRID_W, GRID_W), NEG, F32)
    for ty in range(3):
        for qr in range(ATT_ROWS):
            for kr in range(ATT_WIN):
                out_ref[ty, qr * GRID_W:(qr + 1) * GRID_W, kr * GRID_W:(kr + 1) * GRID_W] = \
                    t_ref[int(dr_idx[ty, qr, kr])]


def _attn_bias_tables(rpb, rows):
    h = rpb.shape[0]
    dr_idx = _bias_row_index(rows)
    qn, kn = ATT_ROWS * GRID_W, ATT_WIN * GRID_W
    return pl.pallas_call(
        functools.partial(_bias_kernel, dr_idx),
        grid=(h,),
        in_specs=[pl.BlockSpec(memory_space=pltpu.SMEM)],
        out_specs=pl.BlockSpec((None, 3, qn, kn), lambda i: (i, 0, 0, 0)),
        out_shape=jax.ShapeDtypeStruct((h, 3, qn, kn), F32),
        scratch_shapes=[pltpu.VMEM((2 * NA_KR, GRID_W, GRID_W), F32)],
        compiler_params=_cparams(("arbitrary",)),
        name="attn_bias_table",
    )(rpb.astype(F32).reshape(-1))


def _attn_kernel(q_ref, k0_ref, k1_ref, k2_ref, k3_ref, kc_ref,
                 v0_ref, v1_ref, v2_ref, v3_ref, vc_ref, tab_ref, o_ref):
    lane = lax.broadcasted_iota(I32, (1, LANES), 1)
    kt_refs = [k0_ref, k1_ref, k2_ref, k3_ref, kc_ref]
    v_refs = [v0_ref, v1_ref, v2_ref, v3_ref, vc_ref]
    kb = ATT_KBLK
    half = q_ref.shape[0] // 2
    hms = [jnp.where((lane >= hh * NA_HEAD_DIM) & (lane < (hh + 1) * NA_HEAD_DIM), 1.0, 0.0).astype(BF16)
           for hh in range(2)]
    hmf = [hm.astype(F32) for hm in hms]
    for pp in range(ATT_HEADS // 2):
        cols = slice(pp * LANES, (pp + 1) * LANES)
        kt = [r[cols, :] for r in kt_refs]
        vv = [r[:, cols] for r in v_refs]
        for r0, blocks in ((0, (0, 1, 2)), (half, (1, 2, 3))):
            q = q_ref[r0:r0 + half, cols]
            qs = jnp.concatenate([q * hms[0], q * hms[1]], axis=0)
            s = [_dot(qs, kt[i])
                 + jnp.concatenate([tab_ref[2 * pp, r0:r0 + half, i * kb:(i + 1) * kb],
                                    tab_ref[2 * pp + 1, r0:r0 + half, i * kb:(i + 1) * kb]], axis=0)
                 for i in blocks]
            s.append(_dot(qs, kt[4]))
            m = jnp.max(s[0], axis=-1, keepdims=True)
            for si in s[1:]:
                m = jnp.maximum(m, jnp.max(si, axis=-1, keepdims=True))
            p = [jnp.exp2(si - m) for si in s]
            l = jnp.sum(p[0], axis=-1, keepdims=True)
            for pi in p[1:]:
                l = l + jnp.sum(pi, axis=-1, keepdims=True)
            o = _dot(p[3].astype(BF16), vv[4])
            for n_, i in enumerate(blocks):
                o = o + _dot(p[n_].astype(BF16), vv[i])
            o = o / l
            acc = o[:half] * hmf[0] + o[half:] * hmf[1]
            o_ref[r0:r0 + half, cols] = acc.astype(o_ref.dtype)


def _attention(q, kt, v, kct, vc, tab, b, s, l):
    n = q.shape[0]
    rows = s // GRID_W
    nb = rows // ATT_ROWS
    qblk = ATT_ROWS * GRID_W
    kpb = s // ATT_KBLK
    hg = NA_HEADS // ATT_HEADS
    hw = ATT_HEADS * NA_HEAD_DIM

    kper = ATT_ROWS * GRID_W // ATT_KBLK
    koff = (NA_KR // 2) * GRID_W // ATT_KBLK

    def kblk(j, i):
        return jnp.clip(kper * j - koff + i, 0, kpb - 1)

    qspec = pl.BlockSpec((qblk, hw), lambda h, j, bb: (bb * nb + j, h))
    ktspecs = [pl.BlockSpec((hw, ATT_KBLK), lambda h, j, bb, i=i: (h, bb * kpb + kblk(j, i)))
               for i in range(4)]
    vspecs = [pl.BlockSpec((ATT_KBLK, hw), lambda h, j, bb, i=i: (bb * kpb + kblk(j, i), h))
              for i in range(4)]
    ctspec = pl.BlockSpec((hw, l), lambda h, j, bb: (h, bb))
    cspec = pl.BlockSpec((l, hw), lambda h, j, bb: (bb, h))
    tspec = pl.BlockSpec((ATT_HEADS, None, qblk, ATT_WIN * GRID_W),
                         lambda h, j, bb: (h, jnp.where(j == 0, 0, jnp.where(j == nb - 1, 2, 1)), 0, 0))
    return pl.pallas_call(
        _attn_kernel,
        grid=(hg, nb, b),
        in_specs=[qspec] + ktspecs + [ctspec] + vspecs + [cspec, tspec],
        out_specs=qspec,
        out_shape=jax.ShapeDtypeStruct((n, NA_WIDTH), BF16),
        compiler_params=_cparams(("arbitrary", "arbitrary", "arbitrary")),
        name="nbr_attention",
    )(q, kt, kt, kt, kt, kct, v, v, v, v, vc, tab)


def _residual_norm_route(x, y, g1, n2g, sh2, sc2, rwt, rb, xo_ref, ho_ref, ri_ref, rf_ref, cnt_ref):
    tm = x.shape[0]
    x1 = x + g1 * y
    xo_ref[...] = x1
    h2 = _norm_mod(x1, n2g, sh2, sc2)
    h2_hi, h2_lo = _split_bf16(h2)
    ho_ref[...] = h2_hi
    w_hi, w_lo = _split_bf16(rwt)
    lg = _dot_nt(jnp.concatenate([w_hi, w_lo], axis=0), h2_hi)
    logits = lg[:N_EXPERTS] + lg[N_EXPERTS:] + _dot_nt(w_hi, h2_lo)
    scores = _sigmoid(logits)
    sel = scores + rb
    r = [sel[e:e + 1, :] for e in range(N_EXPERTS)]
    sc = [scores[e:e + 1, :] for e in range(N_EXPERTS)]
    grp = []
    for g in range(N_GROUPS):
        a, b_, c, d = r[4 * g:4 * g + 4]
        hi1, lo1 = jnp.maximum(a, b_), jnp.minimum(a, b_)
        hi2, lo2 = jnp.maximum(c, d), jnp.minimum(c, d)
        m1 = jnp.maximum(hi1, hi2)
        m2 = jnp.maximum(jnp.minimum(hi1, hi2), jnp.maximum(lo1, lo2))
        grp.append(m1 + m2)
    best = grp[0]
    gi = jnp.zeros(best.shape, I32)
    for g in range(1, N_GROUPS):
        upd = grp[g] > best
        best = jnp.where(upd, grp[g], best)
        gi = jnp.where(upd, g, gi)
    chosen = []
    for e in range(N_EXPERTS):
        g = e // EXPERTS_PER_GROUP
        rank = jnp.zeros(best.shape, I32)
        for e2 in range(EXPERTS_PER_GROUP * g, EXPERTS_PER_GROUP * (g + 1)):
            if e2 == e:
                continue
            ahead = (r[e2] > r[e]) | ((r[e2] == r[e]) & (e2 < e))
            rank = rank + jnp.where(ahead, 1, 0)
        chosen.append((gi == g) & (rank < 2))
    wsel = [jnp.where(chosen[e], sc[e], 0.0) for e in range(N_EXPERTS)]
    denom = wsel[0]
    for e in range(1, N_EXPERTS):
        denom = denom + wsel[e]
    selm = jnp.concatenate([jnp.where(chosen[e], 1.0, 0.0) for e in range(N_EXPERTS)], axis=0)
    ii = lax.broadcasted_iota(I32, (tm, tm), 0)
    jj = lax.broadcasted_iota(I32, (tm, tm), 1)
    tri = jnp.where(ii < jj, 1.0, 0.0).astype(BF16)
    prefix = _dot(selm.astype(BF16), tri)
    cnt = jnp.sum(selm, axis=1, keepdims=True)
    cnt_ref[...] = jnp.broadcast_to(cnt, cnt_ref.shape).astype(I32)
    seen = jnp.zeros(best.shape, I32)
    e0 = jnp.full(best.shape, -1, I32)
    e1 = jnp.full(best.shape, -1, I32)
    r0 = jnp.full(best.shape, -4 * RCOMP, I32)
    r1 = jnp.full(best.shape, -4 * RCOMP, I32)
    g0 = jnp.zeros(best.shape, F32)
    g1_ = jnp.zeros(best.shape, F32)
    for e in range(N_EXPERTS):
        first = chosen[e] & (seen == 0)
        second = chosen[e] & (seen == 1)
        pe = prefix[e:e + 1, :].astype(I32)
        ge = wsel[e] / denom
        e0 = jnp.where(first, e, e0)
        e1 = jnp.where(second, e, e1)
        r0 = jnp.where(first, pe, r0)
        r1 = jnp.where(second, pe, r1)
        g0 = jnp.where(first, ge, g0)
        g1_ = jnp.where(second, ge, g1_)
        seen = seen + jnp.where(chosen[e], 1, 0)
    ri_ref[...] = jnp.zeros(ri_ref.shape, I32)
    rf_ref[...] = jnp.zeros(rf_ref.shape, F32)
    ri_ref[0:1, :] = e0
    ri_ref[1:2, :] = e1
    ri_ref[2:3, :] = r0
    ri_ref[3:4, :] = r1
    rf_ref[0:1, :] = g0
    rf_ref[1:2, :] = g1_


def _tail_specs(n, d, tm, mod, layer, tpb):
    assert tm == TMD
    one = pl.BlockSpec((1, d), lambda t: (0, 0))
    in_specs = [pl.BlockSpec((tm, d), lambda t: (t, 0)),
                mod.vec(layer, 2, tpb), one, mod.vec(layer, 3, tpb), mod.vec(layer, 4, tpb),
                pl.BlockSpec((N_EXPERTS, d), lambda t: (0, 0)),
                pl.BlockSpec((N_EXPERTS, 1), lambda t: (0, 0))]
    out_specs = [pl.BlockSpec((tm, d), lambda t: (t, 0)),
                 pl.BlockSpec((tm, d), lambda t: (t, 0)),
                 pl.BlockSpec((SUBLANES_F32, tm), lambda t: (0, t)),
                 pl.BlockSpec((SUBLANES_F32, tm), lambda t: (0, t)),
                 pl.BlockSpec((None, N_EXPERTS, LANES), lambda t: (t, 0, 0))]
    out_shape = [jax.ShapeDtypeStruct((n, d), F32), jax.ShapeDtypeStruct((n, d), BF16),
                 jax.ShapeDtypeStruct((SUBLANES_F32, n), I32), jax.ShapeDtypeStruct((SUBLANES_F32, n), F32),
                 jax.ShapeDtypeStruct((n // tm, N_EXPERTS, LANES), I32)]
    return in_specs, out_specs, out_shape


def _outproj_kernel(tpb, a_ref, gb_ref, u_ref, up_ref, un_ref, cw_ref, wa_ref, wb_ref,
                    x_ref, g1_ref, n2g_ref, sh2_ref, sc2_ref, rwt_ref, rb_ref,
                    xo_ref, ho_ref, ri_ref, rf_ref, cnt_ref):
    t = pl.program_id(0)
    tm = u_ref.shape[0]
    u = u_ref[...].astype(F32)
    keep_prev = jnp.where(t % tpb == 0, 0.0, 1.0)
    keep_next = jnp.where(t % tpb == tpb - 1, 0.0, 1.0)
    prev_row = up_ref[HALO - 1:HALO, :].astype(F32) * keep_prev
    next_row = un_ref[0:1, :].astype(F32) * keep_next
    row = lax.broadcasted_iota(I32, u.shape, 0)
    u_m1 = jnp.where(row == 0, prev_row, pltpu.roll(u, 1, 0))
    u_p1 = jnp.where(row == tm - 1, next_row, pltpu.roll(u, tm - 1, 0))
    conv = cw_ref[0:1, :] * u_m1 + cw_ref[1:2, :] * u + cw_ref[2:3, :] * u_p1
    bx = (gb_ref[...].astype(F32) * conv).astype(BF16)
    y = _dot(a_ref[...], wa_ref[...]) + _dot(bx, wb_ref[...])
    _residual_norm_route(x_ref[...], y, g1_ref[...], n2g_ref[...], sh2_ref[...], sc2_ref[...],
                         rwt_ref[...], rb_ref[...], xo_ref, ho_ref, ri_ref, rf_ref, cnt_ref)


def _outproj(a, gb, u, cw, wa, wb, x2, mod, n2g, rwt, rb, tpb):
    n, d = x2.shape
    tm = TMD
    w = NA_WIDTH
    hb = tm // HALO
    nh = n // HALO
    half = pl.BlockSpec((tm, w), lambda t: (t, 0))
    tail_in, out_specs, out_shape = _tail_specs(n, d, tm, mod, 0, tpb)
    in_specs = [half, half, half,
                pl.BlockSpec((HALO, w), lambda t: (jnp.maximum(t * hb - 1, 0), 0)),
                pl.BlockSpec((HALO, w), lambda t: (jnp.minimum((t + 1) * hb, nh - 1), 0)),
                pl.BlockSpec(cw.shape, lambda t: (0, 0)),
                pl.BlockSpec(wa.shape, lambda t: (0, 0)),
                pl.BlockSpec(wb.shape, lambda t: (0, 0))] + tail_in
    return pl.pallas_call(
        functools.partial(_outproj_kernel, tpb),
        grid=(n // tm,),
        in_specs=in_specs, out_specs=out_specs, out_shape=out_shape,
        compiler_params=_cparams(("arbitrary",)),
        name="outproj_route",
    )(a, gb, u, u, u, cw, wa, wb, x2, mod.arr, n2g, mod.arr, mod.arr, rwt, rb)


def _conf_kernel(tpb, u_ref, up_ref, un_ref, dww_ref, dwb_ref, lng_ref, lnb_ref, w2_ref, b2_ref,
                 x_ref, g1_ref, n2g_ref, sh2_ref, sc2_ref, rwt_ref, rb_ref,
                 xo_ref, ho_ref, ri_ref, rf_ref, cnt_ref, ue_ref, sh_ref, conv_ref, wb_ref):
    t = pl.program_id(0)
    tm, d = u_ref.shape
    taps = dww_ref.shape[0]
    sl = SUBLANES_F32

    @pl.when(t == 0)
    def _():
        for k in range(taps):
            wb_ref[k * sl:(k + 1) * sl, :] = jnp.broadcast_to(dww_ref[k:k + 1, :], (sl, d))

    keep_prev = jnp.where(t % tpb == 0, 0.0, 1.0)
    keep_next = jnp.where(t % tpb == tpb - 1, 0.0, 1.0)
    ue_ref[0:HALO, :] = up_ref[...].astype(F32) * keep_prev
    ue_ref[HALO:HALO + tm, :] = u_ref[...].astype(F32)
    ue_ref[HALO + tm:HALO + tm + HALO, :] = un_ref[...].astype(F32) * keep_next
    ext = tm + 2 * HALO - sl
    for s in range(1, sl):
        sh_ref[s - 1, :, :] = ue_ref[s:s + ext, :]
    off0 = HALO - (taps - 1) // 2
    nv = CONV_RC // sl
    for cb in range(d // LANES):
        cols = slice(cb * LANES, (cb + 1) * LANES)

        def body(rc, carry, cols=cols):
            r0 = pl.multiple_of(rc * CONV_RC, CONV_RC)
            accs = [None] * nv
            for k in range(taps):
                s, a = (k + off0) % sl, (k + off0) // sl
                wk = wb_ref[k * sl:(k + 1) * sl, cols]
                for j in range(nv):
                    rows = pl.ds(r0 + sl * (a + j), sl)
                    win = ue_ref[rows, cols] if s == 0 else sh_ref[s - 1, rows, cols]
                    term = wk * win
                    accs[j] = term if accs[j] is None else accs[j] + term
            for j in range(nv):
                conv_ref[pl.ds(r0 + sl * j, sl), cols] = accs[j]
            return carry

        lax.fori_loop(0, tm // CONV_RC, body, 0)
    c = conv_ref[...] + dwb_ref[...]
    mu = jnp.mean(c, axis=-1, keepdims=True)
    cc = c - mu
    var = jnp.mean(cc * cc, axis=-1, keepdims=True)
    z = cc * lax.rsqrt(var + EPS) * lng_ref[...] + lnb_ref[...]
    z = (z * _sigmoid(z)).astype(BF16)
    y = _dot(z, w2_ref[...]) + b2_ref[...]
    _residual_norm_route(x_ref[...], y, g1_ref[...], n2g_ref[...], sh2_ref[...], sc2_ref[...],
                         rwt_ref[...], rb_ref[...], xo_ref, ho_ref, ri_ref, rf_ref, cnt_ref)


def _conf(u, dww, dwb, lng, lnb, w2, b2, x2, mod, n2g, rwt, rb, tpb):
    n, d = x2.shape
    tm = TMD
    hb = tm // HALO
    nh = n // HALO
    taps = dww.shape[0]
    assert (taps - 1) // 2 <= HALO - 1 and taps // 2 <= HALO
    one = pl.BlockSpec((1, d), lambda t: (0, 0))
    tail_in, out_specs, out_shape = _tail_specs(n, d, tm, mod, 1, tpb)
    in_specs = [pl.BlockSpec((tm, d), lambda t: (t, 0)),
                pl.BlockSpec((HALO, d), lambda t: (jnp.maximum(t * hb - 1, 0), 0)),
                pl.BlockSpec((HALO, d), lambda t: (jnp.minimum((t + 1) * hb, nh - 1), 0)),
                pl.BlockSpec(dww.shape, lambda t: (0, 0)), one, one, one,
                pl.BlockSpec(w2.shape, lambda t: (0, 0)), one] + tail_in
    ext = tm + 2 * HALO - SUBLANES_F32
    return pl.pallas_call(
        functools.partial(_conf_kernel, tpb),
        grid=(n // tm,),
        in_specs=in_specs, out_specs=out_specs, out_shape=out_shape,
        scratch_shapes=[pltpu.VMEM((tm + 2 * HALO, d), F32),
                        pltpu.VMEM((SUBLANES_F32 - 1, ext, d), F32),
                        pltpu.VMEM((tm, d), F32),
                        pltpu.VMEM((taps * SUBLANES_F32, d), F32)],
        compiler_params=_cparams(("arbitrary",)),
        name="conf_conv_route",
    )(u, u, u, dww, dwb, lng, lnb, w2, b2, x2, mod.arr, n2g, mod.arr, mod.arr, rwt, rb)


def _sorted_rows(n):
    t_n = n // TMD
    rows = 2 * n + (CHUNK - 1) * t_n * N_EXPERTS + N_EXPERTS * (TME - CHUNK)
    return (rows + TME - 1) // TME * TME


def _moe_plan(cnt, n):
    t_n, e_n = cnt.shape
    rt = _sorted_rows(n)
    seg = (cnt + CHUNK - 1) // CHUNK * CHUNK
    lo = jnp.cumsum(seg, axis=1) - seg
    etot = jnp.sum(seg, axis=0)
    epad = (etot + TME - 1) // TME * TME
    eend = jnp.cumsum(epad)
    estart = eend - epad
    go = estart[None, :] + jnp.cumsum(seg, axis=0) - seg
    nchunk = jnp.sum(seg, axis=1) // CHUNK
    e_ids = jnp.arange(e_n, dtype=I32)

    def by_owner(count, slots):
        end = jnp.cumsum(count, axis=1)
        idx = jnp.arange(slots, dtype=I32)
        own = jnp.minimum(jnp.sum((end[:, None, :] <= idx[None, :, None]).astype(I32), axis=-1), e_n - 1)
        pick = lambda a: jnp.sum(jnp.where(own[:, :, None] == e_ids, a[:, None, :], 0), axis=-1)
        return pick, idx[None, :] - pick(end - count)

    segc, lo_c, go_c = seg // CHUNK, lo // CHUNK, go // CHUNK
    pick2, j2 = by_owner(segc // 2, PAIRS_MAX)
    pick1, _ = by_owner(segc % 2, N_EXPERTS)
    nglob = rt // CHUNK
    clist = jnp.concatenate([
        jnp.clip(pick2(lo_c) + 2 * j2, 0, CMAX - 2), jnp.clip(pick2(go_c) + 2 * j2, 0, nglob - 2),
        jnp.clip(pick1(lo_c + segc - 1), 0, CMAX - 1), jnp.clip(pick1(go_c + segc - 1), 0, nglob - 1)],
        axis=1)
    counts = jnp.stack([nchunk, jnp.sum(segc // 2, axis=1), jnp.sum(segc % 2, axis=1)], axis=1)
    npad = (epad - etot) // CHUNK
    pend = jnp.cumsum(npad)
    kk = jnp.arange(e_n * (TME // CHUNK - 1), dtype=I32)
    pown = jnp.minimum(jnp.sum((pend[None, :] <= kk[:, None]).astype(I32), axis=-1), e_n - 1)
    pbase = (estart + etot) // CHUNK - (pend - npad)
    padmap = kk + jnp.sum(jnp.where(pown[:, None] == e_ids, pbase[None, :], 0), axis=-1)
    padmap = jnp.clip(padmap, 0, rt // CHUNK - 1)
    tile_row = jnp.arange(rt // TME, dtype=I32) * TME
    texp = jnp.minimum(jnp.sum((eend[None, :] <= tile_row[:, None]).astype(I32), axis=-1), e_n - 1)
    as_i32 = lambda a: a.astype(I32)
    return dict(lo=as_i32(lo.reshape(-1)), nchunk=as_i32(counts.reshape(-1)), cmap=as_i32(clist.reshape(-1)),
                padmap=as_i32(padmap), npadtot=as_i32(pend[-1:]), texp=as_i32(texp),
                nact=as_i32(eend[-1:] // TME))


def _chunk_copy(src_ref, dst_ref, sem):
    return pltpu.make_async_copy(src_ref, dst_ref, sem)


def _start_tile_copies(counts_ref, clist_ref, tt, make):
    base = tt * COPY_LIST

    def pair(i, c):
        make(clist_ref[base + i], clist_ref[base + PAIRS_MAX + i], 2).start()
        return c

    def single(i, c):
        make(clist_ref[base + 2 * PAIRS_MAX + i], clist_ref[base + 2 * PAIRS_MAX + N_EXPERTS + i], 1).start()
        return c

    lax.fori_loop(0, counts_ref[3 * tt + 1], pair, 0)
    lax.fori_loop(0, counts_ref[3 * tt + 2], single, 0)


def _wait_chunks(n_chunks, desc):
    for bit in range(CMAX.bit_length()):
        @pl.when(((n_chunks >> bit) & 1) == 1)
        def _(bit=bit):
            desc(CHUNK << bit).wait()


def _dispatch_kernel(nchunk_ref, cmap_ref, npad_ref, padmap_ref, nact_ref, lo_ref,
                     h_ref, ri_ref, xs_ref, lp_ref, xc_ref, z_ref, sem, zsem):
    t = pl.program_id(0)
    nt = pl.num_programs(0)
    slot = t % 2
    ntail = xs_ref.shape[0] // TME - nact_ref[0]

    def pad_copy(k):
        g = pl.multiple_of(padmap_ref[k] * CHUNK, CHUNK)
        return _chunk_copy(z_ref.at[0:CHUNK, :], xs_ref.at[pl.ds(g, CHUNK), :], zsem.at[0])

    def tail_copy(k):
        g = pl.multiple_of((nact_ref[0] + k) * TME, TME)
        return _chunk_copy(z_ref, xs_ref.at[pl.ds(g, TME), :], zsem.at[0])

    @pl.when(t == 0)
    def _():
        z_ref[...] = jnp.zeros(z_ref.shape, z_ref.dtype)
        lax.fori_loop(0, npad_ref[0], lambda k, c: (pad_copy(k).start(), c)[1], 0)
        lax.fori_loop(0, ntail, lambda k, c: (tail_copy(k).start(), c)[1], 0)

    e0, e1 = ri_ref[0:1, :], ri_ref[1:2, :]
    b0 = jnp.zeros(e0.shape, I32)
    b1 = jnp.zeros(e0.shape, I32)
    for e in range(N_EXPERTS):
        lo_e = lo_ref[t * N_EXPERTS + e]
        b0 = jnp.where(e0 == e, lo_e, b0)
        b1 = jnp.where(e1 == e, lo_e, b1)
    lp0 = b0 + ri_ref[2:3, :]
    lp1 = b1 + ri_ref[3:4, :]
    lp_ref[...] = jnp.zeros(lp_ref.shape, I32)
    lp_ref[0:1, :] = lp0
    lp_ref[1:2, :] = lp1

    rows = lax.broadcasted_iota(I32, (RCOMP, TMD), 0)
    onehot = jnp.where((lp0 == rows) | (lp1 == rows), 1.0, 0.0).astype(BF16)
    xc_ref[slot] = _dot(onehot, h_ref[...]).astype(BF16)

    def out_copy(lc, gc, k):
        l = pl.multiple_of(lc * CHUNK, CHUNK)
        g = pl.multiple_of(gc * CHUNK, CHUNK)
        return _chunk_copy(xc_ref.at[slot, pl.ds(l, k * CHUNK), :], xs_ref.at[pl.ds(g, k * CHUNK), :],
                           sem.at[slot])

    _start_tile_copies(nchunk_ref, cmap_ref, t, out_copy)

    def out_desc(sl):
        return lambda rows: _chunk_copy(xc_ref.at[sl, 0:rows, :], xs_ref.at[0:rows, :], sem.at[sl])

    @pl.when(t > 0)
    def _():
        _wait_chunks(nchunk_ref[3 * (t - 1)], out_desc(1 - slot))

    @pl.when(t == 0)
    def _():
        lax.fori_loop(0, npad_ref[0], lambda k, c: (pad_copy(k).wait(), c)[1], 0)
        lax.fori_loop(0, ntail, lambda k, c: (tail_copy(k).wait(), c)[1], 0)

    @pl.when(t == nt - 1)
    def _():
        _wait_chunks(nchunk_ref[3 * t], out_desc(slot))


def _dispatch(h2, ri, plan):
    n, d = h2.shape
    t_n = n // TMD
    rt = _sorted_rows(n)
    grid_spec = pltpu.PrefetchScalarGridSpec(
        num_scalar_prefetch=6,
        grid=(t_n,),
        in_specs=[pl.BlockSpec((TMD, d), lambda t, *_: (t, 0)),
                  pl.BlockSpec((SUBLANES_F32, TMD), lambda t, *_: (0, t))],
        out_specs=[pl.BlockSpec(memory_space=pl.ANY),
                   pl.BlockSpec((None, SUBLANES_F32, TMD), lambda t, *_: (t, 0, 0))],
        scratch_shapes=[pltpu.VMEM((2, RCOMP, d), BF16),
                        pltpu.VMEM((TME, d), BF16),
                        pltpu.SemaphoreType.DMA((2,)),
                        pltpu.SemaphoreType.DMA((1,))],
    )
    return pl.pallas_call(
        _dispatch_kernel,
        grid_spec=grid_spec,
        out_shape=[jax.ShapeDtypeStruct((rt, d), BF16),
                   jax.ShapeDtypeStruct((t_n, SUBLANES_F32, TMD), I32)],
        compiler_params=_cparams(("arbitrary",)),
        name="moe_dispatch",
    )(plan["nchunk"], plan["cmap"], plan["npadtot"], plan["padmap"], plan["nact"], plan["lo"], h2, ri)


def _expert_kernel(texp_ref, nact_ref, x_ref, w1_ref, w3_ref, w2_ref, y_ref, w1b, w3b, w2b):
    i = pl.program_id(0)

    @pl.when(i < nact_ref[0])
    def _():
        prev = texp_ref[jnp.maximum(i - 1, 0)]

        @pl.when((i == 0) | (texp_ref[i] != prev))
        def _():
            w1b[...] = w1_ref[...].astype(BF16)
            w3b[...] = w3_ref[...].astype(BF16)
            w2b[...] = w2_ref[...].astype(BF16)

        x = x_ref[...]
        h = _dot(x, w1b[...])
        g = _dot(x, w3b[...])
        a = (h * _sigmoid(h) * g).astype(BF16)
        y_ref[...] = _dot(a, w2b[...]).astype(y_ref.dtype)


def _experts(xs, w1, w3, w2, layer, plan):
    rt, d = xs.shape
    f = w1.shape[-1]
    nt = rt // TME

    def row_map(i, texp, nact):
        return (jnp.clip(i, 0, jnp.maximum(nact[0] - 1, 0)), 0)

    def w_map(i, texp, nact):
        return (layer, texp[jnp.clip(i, 0, jnp.maximum(nact[0] - 1, 0))], 0, 0)

    grid_spec = pltpu.PrefetchScalarGridSpec(
        num_scalar_prefetch=2,
        grid=(nt,),
        in_specs=[pl.BlockSpec((TME, d), row_map),
                  pl.BlockSpec((None, None, d, f), w_map),
                  pl.BlockSpec((None, None, d, f), w_map),
                  pl.BlockSpec((None, None, f, d), w_map)],
        out_specs=pl.BlockSpec((TME, d), row_map),
        scratch_shapes=[pltpu.VMEM((d, f), BF16), pltpu.VMEM((d, f), BF16), pltpu.VMEM((f, d), BF16)],
    )
    return pl.pallas_call(
        _expert_kernel,
        grid_spec=grid_spec,
        out_shape=jax.ShapeDtypeStruct((rt, d), BF16),
        input_output_aliases={2: 0},
        compiler_params=_cparams(("arbitrary",)),
        name="moe_experts",
    )(plan["texp"], plan["nact"], xs, w1, w3, w2)


def _combine_core(nchunk_ref, cmap_ref, ys_ref, ci_ref, cf_ref, x_ref, g2_ref, yc_ref, sem):
    t = pl.program_id(0)
    nt = pl.num_programs(0)
    slot = t % 2

    def in_copy(sl):
        def make(lc, gc, k):
            l = pl.multiple_of(lc * CHUNK, CHUNK)
            g = pl.multiple_of(gc * CHUNK, CHUNK)
            return _chunk_copy(ys_ref.at[pl.ds(g, k * CHUNK), :], yc_ref.at[sl, pl.ds(l, k * CHUNK), :],
                               sem.at[sl])
        return make

    @pl.when(t == 0)
    def _():
        yc_ref[...] = jnp.zeros(yc_ref.shape, yc_ref.dtype)
        _start_tile_copies(nchunk_ref, cmap_ref, 0, in_copy(0))

    @pl.when(t + 1 < nt)
    def _():
        _start_tile_copies(nchunk_ref, cmap_ref, t + 1, in_copy(1 - slot))

    _wait_chunks(nchunk_ref[3 * t],
                 lambda rows: _chunk_copy(ys_ref.at[0:rows, :], yc_ref.at[slot, 0:rows, :], sem.at[slot]))

    ci = ci_ref[...]
    cf = cf_ref[...]
    cols = lax.broadcasted_iota(I32, (TMD, RCOMP), 1)
    w = jnp.where(ci[:, 0:1] == cols, cf[:, 0:1], 0.0) + jnp.where(ci[:, 1:2] == cols, cf[:, 1:2], 0.0)
    moe = _dot(w.astype(BF16), yc_ref[slot])
    return x_ref[...] + g2_ref[...] * moe


def _combine_final_kernel(nchunk_ref, cmap_ref, ys_ref, ci_ref, cf_ref, x_ref, g2_ref, fg_ref,
                          o_ref, yc_ref, sem):
    x1 = _combine_core(nchunk_ref, cmap_ref, ys_ref, ci_ref, cf_ref, x_ref, g2_ref, yc_ref, sem)
    ms = jnp.mean(x1 * x1, axis=-1, keepdims=True)
    o_ref[...] = x1 * lax.rsqrt(ms + EPS) * fg_ref[...]


def _combine_pw1_kernel(nchunk_ref, cmap_ref, ys_ref, ci_ref, cf_ref, x_ref, g2_ref,
                        n1g_ref, sh_ref, sc_ref, w_ref, b_ref, o_ref, u_ref, yc_ref, sem):
    x1 = _combine_core(nchunk_ref, cmap_ref, ys_ref, ci_ref, cf_ref, x_ref, g2_ref, yc_ref, sem)
    o_ref[...] = x1
    d = x1.shape[1]
    h = _norm_mod(x1, n1g_ref[...], sh_ref[...], sc_ref[...]).astype(BF16)
    a = _dot(h, w_ref[:, 0:d]) + b_ref[:, 0:d]
    g = _dot(h, w_ref[:, d:2 * d]) + b_ref[:, d:2 * d]
    u_ref[...] = (a * _sigmoid(g)).astype(BF16)


def _combine(ys, plan, lp, rf, x2, mod, layer, tpb, tail):
    n, d = x2.shape
    t_n = n // TMD
    col = pl.BlockSpec((TMD, SUBLANES_F32), lambda t, *_: (t, 0))
    one = pl.BlockSpec((1, d), lambda t, *_: (0, 0))
    tile = pl.BlockSpec((TMD, d), lambda t, *_: (t, 0))
    in_specs = [pl.BlockSpec(memory_space=pl.ANY), col, col, tile, mod.vec(layer, 5, tpb)]
    args = [plan["nchunk"], plan["cmap"], ys,
            jnp.transpose(lp, (0, 2, 1)).reshape(n, SUBLANES_F32), rf.T, x2, mod.arr]
    if tail[0] == "final":
        body, name = _combine_final_kernel, "moe_combine_final"
        in_specs += [one]
        args += [tail[1]]
        out_specs = tile
        out_shape = jax.ShapeDtypeStruct((n, d), F32)
    else:
        body, name = _combine_pw1_kernel, "moe_combine_pw1"
        _, n1g, w_bf, bias = tail
        in_specs += [one, mod.vec(layer + 1, 0, tpb), mod.vec(layer + 1, 1, tpb),
                     pl.BlockSpec(w_bf.shape, lambda t, *_: (0, 0)),
                     pl.BlockSpec(bias.shape, lambda t, *_: (0, 0))]
        args += [n1g, mod.arr, mod.arr, w_bf, bias]
        out_specs = [tile, tile]
        out_shape = [jax.ShapeDtypeStruct((n, d), F32), jax.ShapeDtypeStruct((n, d), BF16)]
    grid_spec = pltpu.PrefetchScalarGridSpec(
        num_scalar_prefetch=2,
        grid=(t_n,),
        in_specs=in_specs,
        out_specs=out_specs,
        scratch_shapes=[pltpu.VMEM((2, RCOMP, d), BF16), pltpu.SemaphoreType.DMA((2,))],
    )
    return pl.pallas_call(body, grid_spec=grid_spec, out_shape=out_shape,
                          compiler_params=_cparams(("arbitrary",)), name=name)(*args)


def _moe(x1, h2, ri, rf, cnt, w1, w3, w2, layer, mod, tpb, tail):
    n = x1.shape[0]
    plan = _moe_plan(cnt[:, :, 0], n)
    xs, lp = _dispatch(h2, ri, plan)
    ys = _experts(xs, w1, w3, w2, layer, plan)
    return _combine(ys, plan, lp, rf, x1, mod, layer, tpb, tail)


def kernel(x, c, ctx, c_ctx, norm1_g, norm2_g, w_mod, b_mod, na_w_in, na_rpb, sc_conv_w, ab_w_out,
           cf_pw1_w, cf_pw1_b, cf_dw_w, cf_dw_b, cf_ln_g, cf_ln_b, cf_pw2_w, cf_pw2_b,
           router_w, router_b, moe_w1, moe_w3, moe_w2, final_g):
    b, s, d = x.shape
    l = ctx.shape[1]
    n = b * s
    depth = w_mod.shape[0]
    assert depth == 2, "layer pattern (attention layer, Conformer layer) is written out for depth 2"
    assert s % TM_PROJ == 0 and s % TMD == 0
    rows = s // GRID_W
    assert rows % ATT_ROWS == 0 and rows >= ATT_WIN and s % ATT_KBLK == 0
    tpb = s // TM_PROJ
    tpb_d = s // TMD

    mr = (b + 1 + SUBLANES_F32 - 1) // SUBLANES_F32 * SUBLANES_F32
    c_all = jnp.concatenate([c, c_ctx[None, :], jnp.zeros((mr - b - 1, d), F32)], axis=0)
    mod = _Mod(_modulation(c_all, w_mod, b_mod), d)

    x2 = x.reshape(n, d)
    rwt = router_w.T
    rb = router_b.reshape(N_EXPERTS, 1)
    fg = final_g.reshape(1, d)

    w_in = na_w_in[0].astype(BF16)
    n1g = norm1_g[0].reshape(1, d)
    wkt = w_in[:, NA_WIDTH:2 * NA_WIDTH].T
    q, kt, v, gb, u = _inproj(x2, n1g, mod, w_in, wkt, tpb)
    kct, vc = _ctxproj(ctx.reshape(b * l, d), n1g, mod, b, wkt, w_in[:, 2 * NA_WIDTH:3 * NA_WIDTH], l)
    tab = _attn_bias_tables(na_rpb[0], rows)
    a = _attention(q, kt, v, kct, vc, tab, b, s, l)
    w_out = ab_w_out[0].astype(BF16)
    x1, h2, ri, rf, cnt = _outproj(a, gb, u, sc_conv_w[0], w_out[:NA_WIDTH], w_out[NA_WIDTH:], x2,
                                   mod, norm2_g[0].reshape(1, d), rwt, rb, tpb_d)
    x2, u = _moe(x1, h2, ri, rf, cnt, moe_w1, moe_w3, moe_w2, 0, mod, tpb_d,
                 ("pw1", norm1_g[1].reshape(1, d), cf_pw1_w[0].astype(BF16), cf_pw1_b[0].reshape(1, 2 * d)))
    x1, h2, ri, rf, cnt = _conf(u, cf_dw_w[0], cf_dw_b[0].reshape(1, d), cf_ln_g[0].reshape(1, d),
                                cf_ln_b[0].reshape(1, d), cf_pw2_w[0].astype(BF16),
                                cf_pw2_b[0].reshape(1, d), x2, mod, norm2_g[1].reshape(1, d),
                                rwt, rb, tpb_d)
    out = _moe(x1, h2, ri, rf, cnt, moe_w1, moe_w3, moe_w2, 1, mod, tpb_d, ("final", fg))
    return out.reshape(b, s, d)
```

```python
import functools

import numpy as np
import jax
import jax.numpy as jnp
from jax import lax
from jax.experimental import pallas as pl
from jax.experimental.pallas import tpu as pltpu

F32 = jnp.float32
BF16 = jnp.bfloat16
I32 = jnp.int32

GRID_W = 64
NA_HEADS = 8
NA_HEAD_DIM = 64
NA_WIDTH = NA_HEADS * NA_HEAD_DIM
NA_KR = 8
NA_KC = 16
N_EXPERTS = 16
N_GROUPS = 4
EXPERTS_PER_GROUP = N_EXPERTS // N_GROUPS
N_MOD = 6
EPS = 1e-6
NEG = -1e30
LOG2E = 1.4426950408889634

LANES = 128
SUBLANES_F32 = 8
SUBLANES_BF16 = 16

TM_PROJ = 1024
ATT_ROWS = 8
ATT_WIN = 2 * ATT_ROWS
ATT_KBLK = 256
ATT_HEADS = 8
TMD = 512
CHUNK = SUBLANES_BF16
CMAX = (2 * TMD + (N_EXPERTS - 1) * CHUNK + CHUNK - 1) // CHUNK + 1
RCOMP = CMAX * CHUNK
PAIRS_MAX = CMAX // 2
COPY_LIST = 2 * PAIRS_MAX + 2 * N_EXPERTS
TME = 1024
CONV_RC = 128
HALO = SUBLANES_BF16
VMEM_LIMIT = 52 * 1024 * 1024


def _cparams(sem):
    return pltpu.CompilerParams(dimension_semantics=sem, vmem_limit_bytes=VMEM_LIMIT)


def _dot(a, b):
    return jnp.dot(a, b, preferred_element_type=F32)


def _dot_nt(a, b, precision=None):
    return lax.dot_general(a, b, (((1,), (1,)), ((), ())), precision=precision,
                           preferred_element_type=F32)


def _sigmoid(x):
    return 1.0 / (1.0 + jnp.exp(-x))


def _norm_mod(x, g, sh, sc):
    ms = jnp.mean(x * x, axis=-1, keepdims=True)
    y = x * lax.rsqrt(ms + EPS) * g
    return y * (1.0 + sc) + sh


def _split_bf16(a):
    hi = a.astype(BF16)
    lo = (a - hi.astype(F32)).astype(BF16)
    return hi, lo


def _mod_kernel(c_ref, w_ref, b_ref, o_ref):
    c = c_ref[...]
    mr = c.shape[0]
    hi, lo = _split_bf16(c * _sigmoid(c))
    w_hi, w_lo = _split_bf16(w_ref[...])
    r = _dot(jnp.concatenate([hi, lo], axis=0), w_hi)
    o_ref[...] = r[:mr] + r[mr:] + _dot(hi, w_lo) + b_ref[...]


def _modulation(c_all, w_mod, b_mod):
    depth, d, n6 = w_mod.shape
    mr = c_all.shape[0]
    tn = 2048
    return pl.pallas_call(
        _mod_kernel,
        grid=(depth, n6 // tn),
        in_specs=[pl.BlockSpec((mr, d), lambda i, j: (0, 0)),
                  pl.BlockSpec((None, d, tn), lambda i, j: (i, 0, j)),
                  pl.BlockSpec((None, 1, tn), lambda i, j: (i, 0, j))],
        out_specs=pl.BlockSpec((None, mr, tn), lambda i, j: (i, 0, j)),
        out_shape=jax.ShapeDtypeStruct((depth, mr, n6), F32),
        compiler_params=_cparams(("arbitrary", "arbitrary")),
        name="modulation",
    )(c_all, w_mod, b_mod.reshape(depth, 1, n6))


class _Mod:
    def __init__(self, mod, d):
        self.depth, self.mr, _ = mod.shape
        self.d = d
        self.arr = mod.reshape(self.depth * self.mr * N_MOD, 1, d)

    def vec(self, layer, j, tpb):
        mr = self.mr
        return pl.BlockSpec((None, 1, self.d),
                            lambda t, *_: ((layer * mr + t // tpb) * N_MOD + j, 0, 0))

    def row(self, layer, j, r):
        mr = self.mr
        return pl.BlockSpec((None, 1, self.d), lambda t, *_: ((layer * mr + r) * N_MOD + j, 0, 0))


def _inproj_kernel(x_ref, g_ref, sh_ref, sc_ref, w_ref, wkt_ref, q_ref, kt_ref, v_ref, gb_ref, u_ref):
    h = _norm_mod(x_ref[...], g_ref[...], sh_ref[...], sc_ref[...]).astype(BF16)
    w = NA_WIDTH
    q_ref[...] = (_dot(h, w_ref[:, 0:w]) * (NA_HEAD_DIM ** -0.5 * LOG2E)).astype(BF16)
    kt_ref[...] = _dot_nt(wkt_ref[...], h).astype(BF16)
    v_ref[...] = _dot(h, w_ref[:, 2 * w:3 * w]).astype(BF16)
    gb_ref[...] = _dot(h, w_ref[:, 3 * w:4 * w]).astype(BF16)
    u_ref[...] = (_dot(h, w_ref[:, 4 * w:5 * w]) * _dot(h, w_ref[:, 5 * w:6 * w])).astype(BF16)


def _inproj(x2, g, mod, w_bf, wkt_bf, tpb):
    n, d = x2.shape
    tm = TM_PROJ
    out = jax.ShapeDtypeStruct((n, NA_WIDTH), BF16)
    ospec = pl.BlockSpec((tm, NA_WIDTH), lambda t: (t, 0))
    out_t = jax.ShapeDtypeStruct((NA_WIDTH, n), BF16)
    ospec_t = pl.BlockSpec((NA_WIDTH, tm), lambda t: (0, t))
    return pl.pallas_call(
        _inproj_kernel,
        grid=(n // tm,),
        in_specs=[pl.BlockSpec((tm, d), lambda t: (t, 0)),
                  pl.BlockSpec((1, d), lambda t: (0, 0)), mod.vec(0, 0, tpb), mod.vec(0, 1, tpb),
                  pl.BlockSpec(w_bf.shape, lambda t: (0, 0)),
                  pl.BlockSpec(wkt_bf.shape, lambda t: (0, 0))],
        out_specs=[ospec, ospec_t, ospec, ospec, ospec],
        out_shape=[out, out_t, out, out, out],
        compiler_params=_cparams(("arbitrary",)),
        name="inproj",
    )(x2, g, mod.arr, mod.arr, w_bf, wkt_bf)


def _ctxproj_kernel(x_ref, g_ref, sh_ref, sc_ref, wkt_ref, wv_ref, kt_ref, v_ref):
    h = _norm_mod(x_ref[...], g_ref[...], sh_ref[...], sc_ref[...]).astype(BF16)
    kt_ref[...] = _dot_nt(wkt_ref[...], h).astype(BF16)
    v_ref[...] = _dot(h, wv_ref[...]).astype(BF16)


def _ctxproj(ctx2, g, mod, ctx_row, wkt_bf, wv_bf, l):
    n, d = ctx2.shape
    return pl.pallas_call(
        _ctxproj_kernel,
        grid=(n // l,),
        in_specs=[pl.BlockSpec((l, d), lambda t: (t, 0)), pl.BlockSpec((1, d), lambda t: (0, 0)),
                  mod.row(0, 0, ctx_row), mod.row(0, 1, ctx_row),
                  pl.BlockSpec(wkt_bf.shape, lambda t: (0, 0)),
                  pl.BlockSpec(wv_bf.shape, lambda t: (0, 0))],
        out_specs=[pl.BlockSpec((NA_WIDTH, l), lambda t: (0, t)),
                   pl.BlockSpec((l, NA_WIDTH), lambda t: (t, 0))],
        out_shape=[jax.ShapeDtypeStruct((NA_WIDTH, n), BF16), jax.ShapeDtypeStruct((n, NA_WIDTH), BF16)],
        compiler_params=_cparams(("arbitrary",)),
        name="ctxproj",
    )(ctx2, g, mod.arr, mod.arr, wkt_bf, wv_bf)


def _bias_row_index(rows):
    nb = rows // ATT_ROWS
    n_dr = 2 * NA_KR - 1
    dr_idx = np.full((3, ATT_ROWS, ATT_WIN), n_dr, np.int32)
    for ty, j in enumerate((0, 1, nb - 1)):
        ws = ATT_ROWS * j - NA_KR // 2
        for qr in range(ATT_ROWS):
            r = ATT_ROWS * j + qr
            rs = int(np.clip(r - NA_KR // 2, 0, rows - NA_KR))
            for kr in range(ATT_WIN):
                key_row = ws + kr
                if rs <= key_row < rs + NA_KR:
                    dr_idx[ty, qr, kr] = key_row - r + (NA_KR - 1)
    kr_blk = ATT_KBLK // GRID_W
    assert (dr_idx[:, :ATT_ROWS // 2, ATT_WIN - kr_blk:] == n_dr).all()
    assert (dr_idx[:, ATT_ROWS // 2:, :kr_blk] == n_dr).all()
    return dr_idx


def _bias_kernel(dr_idx, rpb_ref, out_ref, t_ref):
    h = pl.program_id(0)
    n_dr = 2 * NA_KR - 1
    n_dc = 2 * NA_KC - 1
    per_vreg = LANES // GRID_W
    assert per_vreg == 2 and ATT_WIN % per_vreg == 0
    qc = lax.broadcasted_iota(I32, (GRID_W, LANES), 0)
    lane = lax.broadcasted_iota(I32, (GRID_W, LANES), 1)
    kc = jnp.bitwise_and(lane, GRID_W - 1)
    cs = jnp.clip(qc - NA_KC // 2, 0, GRID_W - NA_KC)
    valid = (kc >= cs) & (kc < cs + NA_KC)
    dc = jnp.where(valid, kc - qc + (NA_KC - 1), -1)
    for dr in range(n_dr):
        slab = jnp.full((GRID_W, LANES), NEG, F32)
        for m in range(n_dc):
            slab = jnp.where(dc == m, rpb_ref[(h * n_dr + dr) * n_dc + m] * LOG2E, slab)
        t_ref[dr] = slab
    t_ref[n_dr] = jnp.full((GRID_W, LANES), NEG, F32)
    first = lane < GRID_W
    for ty in range(3):
        for qr in range(ATT_ROWS):
            for kp in range(ATT_WIN // per_vreg):
                d0, d1 = int(dr_idx[ty, qr, 2 * kp]), int(dr_idx[ty, qr, 2 * kp + 1])
                pair = t_ref[d0] if d0 == d1 else jnp.where(first, t_ref[d0], t_ref[d1])
                out_ref[ty, qr * GRID_W:(qr + 1) * GRID_W, kp * LANES:(kp + 1) * LANES] = pair


def _attn_bias_tables(rpb, rows):
    h = rpb.shape[0]
    dr_idx = _bias_row_index(rows)
    qn, kn = ATT_ROWS * GRID_W, ATT_WIN * GRID_W
    return pl.pallas_call(
        functools.partial(_bias_kernel, dr_idx),
        grid=(h,),
        in_specs=[pl.BlockSpec(memory_space=pltpu.SMEM)],
        out_specs=pl.BlockSpec((None, 3, qn, kn), lambda i: (i, 0, 0, 0)),
        out_shape=jax.ShapeDtypeStruct((h, 3, qn, kn), F32),
        scratch_shapes=[pltpu.VMEM((2 * NA_KR, GRID_W, LANES), F32)],
        compiler_params=_cparams(("arbitrary",)),
        name="attn_bias_table",
    )(rpb.astype(F32).reshape(-1))


def _attn_kernel(q_ref, k0_ref, k1_ref, k2_ref, k3_ref, kc_ref,
                 v0_ref, v1_ref, v2_ref, v3_ref, vc_ref, tab_ref, o_ref):
    lane = lax.broadcasted_iota(I32, (1, LANES), 1)
    kt_refs = [k0_ref, k1_ref, k2_ref, k3_ref, kc_ref]
    v_refs = [v0_ref, v1_ref, v2_ref, v3_ref, vc_ref]
    kb = ATT_KBLK
    half = q_ref.shape[0] // 2
    hms = [jnp.where((lane >= hh * NA_HEAD_DIM) & (lane < (hh + 1) * NA_HEAD_DIM), 1.0, 0.0).astype(BF16)
           for hh in range(2)]
    hmf = [hm.astype(F32) for hm in hms]
    for pp in range(ATT_HEADS // 2):
        cols = slice(pp * LANES, (pp + 1) * LANES)
        kt = [r[cols, :] for r in kt_refs]
        vv = [r[:, cols] for r in v_refs]
        for r0, blocks in ((0, (0, 1, 2)), (half, (1, 2, 3))):
            q = q_ref[r0:r0 + half, cols]
            qs = jnp.concatenate([q * hms[0], q * hms[1]], axis=0)
            s = [_dot(qs, kt[i])
                 + jnp.concatenate([tab_ref[2 * pp, r0:r0 + half, i * kb:(i + 1) * kb],
                                    tab_ref[2 * pp + 1, r0:r0 + half, i * kb:(i + 1) * kb]], axis=0)
                 for i in blocks]
            s.append(_dot(qs, kt[4]))
            m = jnp.max(s[0], axis=-1, keepdims=True)
            for si in s[1:]:
                m = jnp.maximum(m, jnp.max(si, axis=-1, keepdims=True))
            p = [jnp.exp2(si - m) for si in s]
            l = jnp.sum(p[0], axis=-1, keepdims=True)
            for pi in p[1:]:
                l = l + jnp.sum(pi, axis=-1, keepdims=True)
            o = _dot(p[3].astype(BF16), vv[4])
            for n_, i in enumerate(blocks):
                o = o + _dot(p[n_].astype(BF16), vv[i])
            o = o / l
            acc = o[:half] * hmf[0] + o[half:] * hmf[1]
            o_ref[r0:r0 + half, cols] = acc.astype(o_ref.dtype)


def _attention(q, kt, v, kct, vc, tab, b, s, l):
    n = q.shape[0]
    rows = s // GRID_W
    nb = rows // ATT_ROWS
    qblk = ATT_ROWS * GRID_W
    kpb = s // ATT_KBLK
    hg = NA_HEADS // ATT_HEADS
    hw = ATT_HEADS * NA_HEAD_DIM

    kper = ATT_ROWS * GRID_W // ATT_KBLK
    koff = (NA_KR // 2) * GRID_W // ATT_KBLK

    def kblk(j, i):
        return jnp.clip(kper * j - koff + i, 0, kpb - 1)

    qspec = pl.BlockSpec((qblk, hw), lambda h, j, bb: (bb * nb + j, h))
    ktspecs = [pl.BlockSpec((hw, ATT_KBLK), lambda h, j, bb, i=i: (h, bb * kpb + kblk(j, i)))
               for i in range(4)]
    vspecs = [pl.BlockSpec((ATT_KBLK, hw), lambda h, j, bb, i=i: (bb * kpb + kblk(j, i), h))
              for i in range(4)]
    ctspec = pl.BlockSpec((hw, l), lambda h, j, bb: (h, bb))
    cspec = pl.BlockSpec((l, hw), lambda h, j, bb: (bb, h))
    tspec = pl.BlockSpec((ATT_HEADS, None, qblk, ATT_WIN * GRID_W),
                         lambda h, j, bb: (h, jnp.where(j == 0, 0, jnp.where(j == nb - 1, 2, 1)), 0, 0))
    return pl.pallas_call(
        _attn_kernel,
        grid=(hg, nb, b),
        in_specs=[qspec] + ktspecs + [ctspec] + vspecs + [cspec, tspec],
        out_specs=qspec,
        out_shape=jax.ShapeDtypeStruct((n, NA_WIDTH), BF16),
        compiler_params=_cparams(("arbitrary", "arbitrary", "arbitrary")),
        name="nbr_attention",
    )(q, kt, kt, kt, kt, kct, v, v, v, v, vc, tab)


def _residual_norm_route(x, y, g1, n2g, sh2, sc2, rwt, rb, xo_ref, ho_ref, ri_ref, rf_ref, cnt_ref):
    tm = x.shape[0]
    x1 = x + g1 * y
    xo_ref[...] = x1
    h2 = _norm_mod(x1, n2g, sh2, sc2)
    h2_hi, h2_lo = _split_bf16(h2)
    ho_ref[...] = h2_hi
    w_hi, w_lo = _split_bf16(rwt)
    lg = _dot_nt(jnp.concatenate([w_hi, w_lo], axis=0), h2_hi)
    logits = lg[:N_EXPERTS] + lg[N_EXPERTS:] + _dot_nt(w_hi, h2_lo)
    scores = _sigmoid(logits)
    sel = scores + rb
    r = [sel[e:e + 1, :] for e in range(N_EXPERTS)]
    sc = [scores[e:e + 1, :] for e in range(N_EXPERTS)]
    grp = []
    for g in range(N_GROUPS):
        a, b_, c, d = r[4 * g:4 * g + 4]
        hi1, lo1 = jnp.maximum(a, b_), jnp.minimum(a, b_)
        hi2, lo2 = jnp.maximum(c, d), jnp.minimum(c, d)
        m1 = jnp.maximum(hi1, hi2)
        m2 = jnp.maximum(jnp.minimum(hi1, hi2), jnp.maximum(lo1, lo2))
        grp.append(m1 + m2)
    best = grp[0]
    gi = jnp.zeros(best.shape, I32)
    for g in range(1, N_GROUPS):
        upd = grp[g] > best
        best = jnp.where(upd, grp[g], best)
        gi = jnp.where(upd, g, gi)
    chosen = []
    for e in range(N_EXPERTS):
        g = e // EXPERTS_PER_GROUP
        rank = jnp.zeros(best.shape, I32)
        for e2 in range(EXPERTS_PER_GROUP * g, EXPERTS_PER_GROUP * (g + 1)):
            if e2 == e:
                continue
            ahead = (r[e2] > r[e]) | ((r[e2] == r[e]) & (e2 < e))
            rank = rank + jnp.where(ahead, 1, 0)
        chosen.append((gi == g) & (rank < 2))
    wsel = [jnp.where(chosen[e], sc[e], 0.0) for e in range(N_EXPERTS)]
    denom = wsel[0]
    for e in range(1, N_EXPERTS):
        denom = denom + wsel[e]
    selm = jnp.concatenate([jnp.where(chosen[e], 1.0, 0.0) for e in range(N_EXPERTS)], axis=0)
    ii = lax.broadcasted_iota(I32, (tm, tm), 0)
    jj = lax.broadcasted_iota(I32, (tm, tm), 1)
    tri = jnp.where(ii < jj, 1.0, 0.0).astype(BF16)
    prefix = _dot(selm.astype(BF16), tri)
    cnt = jnp.sum(selm, axis=1, keepdims=True)
    cnt_ref[...] = jnp.broadcast_to(cnt, cnt_ref.shape).astype(I32)
    seen = jnp.zeros(best.shape, I32)
    e0 = jnp.full(best.shape, -1, I32)
    e1 = jnp.full(best.shape, -1, I32)
    r0 = jnp.full(best.shape, -4 * RCOMP, I32)
    r1 = jnp.full(best.shape, -4 * RCOMP, I32)
    g0 = jnp.zeros(best.shape, F32)
    g1_ = jnp.zeros(best.shape, F32)
    for e in range(N_EXPERTS):
        first = chosen[e] & (seen == 0)
        second = chosen[e] & (seen == 1)
        pe = prefix[e:e + 1, :].astype(I32)
        ge = wsel[e] / denom
        e0 = jnp.where(first, e, e0)
        e1 = jnp.where(second, e, e1)
        r0 = jnp.where(first, pe, r0)
        r1 = jnp.where(second, pe, r1)
        g0 = jnp.where(first, ge, g0)
        g1_ = jnp.where(second, ge, g1_)
        seen = seen + jnp.where(chosen[e], 1, 0)
    ri_ref[...] = jnp.zeros(ri_ref.shape, I32)
    rf_ref[...] = jnp.zeros(rf_ref.shape, F32)
    ri_ref[0:1, :] = e0
    ri_ref[1:2, :] = e1
    ri_ref[2:3, :] = r0
    ri_ref[3:4, :] = r1
    rf_ref[0:1, :] = g0
    rf_ref[1:2, :] = g1_


def _tail_specs(n, d, tm, mod, layer, tpb):
    assert tm == TMD
    one = pl.BlockSpec((1, d), lambda t: (0, 0))
    in_specs = [pl.BlockSpec((tm, d), lambda t: (t, 0)),
                mod.vec(layer, 2, tpb), one, mod.vec(layer, 3, tpb), mod.vec(layer, 4, tpb),
                pl.BlockSpec((N_EXPERTS, d), lambda t: (0, 0)),
                pl.BlockSpec((N_EXPERTS, 1), lambda t: (0, 0))]
    out_specs = [pl.BlockSpec((tm, d), lambda t: (t, 0)),
                 pl.BlockSpec((tm, d), lambda t: (t, 0)),
                 pl.BlockSpec((SUBLANES_F32, tm), lambda t: (0, t)),
                 pl.BlockSpec((SUBLANES_F32, tm), lambda t: (0, t)),
                 pl.BlockSpec((None, N_EXPERTS, LANES), lambda t: (t, 0, 0))]
    out_shape = [jax.ShapeDtypeStruct((n, d), F32), jax.ShapeDtypeStruct((n, d), BF16),
                 jax.ShapeDtypeStruct((SUBLANES_F32, n), I32), jax.ShapeDtypeStruct((SUBLANES_F32, n), F32),
                 jax.ShapeDtypeStruct((n // tm, N_EXPERTS, LANES), I32)]
    return in_specs, out_specs, out_shape


def _outproj_kernel(tpb, a_ref, gb_ref, u_ref, up_ref, un_ref, cw_ref, wa_ref, wb_ref,
                    x_ref, g1_ref, n2g_ref, sh2_ref, sc2_ref, rwt_ref, rb_ref,
                    xo_ref, ho_ref, ri_ref, rf_ref, cnt_ref):
    t = pl.program_id(0)
    tm = u_ref.shape[0]
    u = u_ref[...].astype(F32)
    keep_prev = jnp.where(t % tpb == 0, 0.0, 1.0)
    keep_next = jnp.where(t % tpb == tpb - 1, 0.0, 1.0)
    prev_row = up_ref[HALO - 1:HALO, :].astype(F32) * keep_prev
    next_row = un_ref[0:1, :].astype(F32) * keep_next
    row = lax.broadcasted_iota(I32, u.shape, 0)
    u_m1 = jnp.where(row == 0, prev_row, pltpu.roll(u, 1, 0))
    u_p1 = jnp.where(row == tm - 1, next_row, pltpu.roll(u, tm - 1, 0))
    conv = cw_ref[0:1, :] * u_m1 + cw_ref[1:2, :] * u + cw_ref[2:3, :] * u_p1
    bx = (gb_ref[...].astype(F32) * conv).astype(BF16)
    y = _dot(a_ref[...], wa_ref[...]) + _dot(bx, wb_ref[...])
    _residual_norm_route(x_ref[...], y, g1_ref[...], n2g_ref[...], sh2_ref[...], sc2_ref[...],
                         rwt_ref[...], rb_ref[...], xo_ref, ho_ref, ri_ref, rf_ref, cnt_ref)


def _outproj(a, gb, u, cw, wa, wb, x2, mod, n2g, rwt, rb, tpb):
    n, d = x2.shape
    tm = TMD
    w = NA_WIDTH
    hb = tm // HALO
    nh = n // HALO
    half = pl.BlockSpec((tm, w), lambda t: (t, 0))
    tail_in, out_specs, out_shape = _tail_specs(n, d, tm, mod, 0, tpb)
    in_specs = [half, half, half,
                pl.BlockSpec((HALO, w), lambda t: (jnp.maximum(t * hb - 1, 0), 0)),
                pl.BlockSpec((HALO, w), lambda t: (jnp.minimum((t + 1) * hb, nh - 1), 0)),
                pl.BlockSpec(cw.shape, lambda t: (0, 0)),
                pl.BlockSpec(wa.shape, lambda t: (0, 0)),
                pl.BlockSpec(wb.shape, lambda t: (0, 0))] + tail_in
    return pl.pallas_call(
        functools.partial(_outproj_kernel, tpb),
        grid=(n // tm,),
        in_specs=in_specs, out_specs=out_specs, out_shape=out_shape,
        compiler_params=_cparams(("arbitrary",)),
        name="outproj_route",
    )(a, gb, u, u, u, cw, wa, wb, x2, mod.arr, n2g, mod.arr, mod.arr, rwt, rb)


def _conf_kernel(tpb, u_ref, up_ref, un_ref, dww_ref, dwb_ref, lng_ref, lnb_ref, w2_ref, b2_ref,
                 x_ref, g1_ref, n2g_ref, sh2_ref, sc2_ref, rwt_ref, rb_ref,
                 xo_ref, ho_ref, ri_ref, rf_ref, cnt_ref, ue_ref, sh_ref, conv_ref, wb_ref):
    t = pl.program_id(0)
    tm, d = u_ref.shape
    taps = dww_ref.shape[0]
    sl = SUBLANES_F32

    @pl.when(t == 0)
    def _():
        for k in range(taps):
            wb_ref[k * sl:(k + 1) * sl, :] = jnp.broadcast_to(dww_ref[k:k + 1, :], (sl, d))

    keep_prev = jnp.where(t % tpb == 0, 0.0, 1.0)
    keep_next = jnp.where(t % tpb == tpb - 1, 0.0, 1.0)
    ue_ref[0:HALO, :] = up_ref[...].astype(F32) * keep_prev
    ue_ref[HALO:HALO + tm, :] = u_ref[...].astype(F32)
    ue_ref[HALO + tm:HALO + tm + HALO, :] = un_ref[...].astype(F32) * keep_next
    ext = tm + 2 * HALO - sl
    for s in range(1, sl):
        sh_ref[s - 1, :, :] = ue_ref[s:s + ext, :]
    off0 = HALO - (taps - 1) // 2
    nv = CONV_RC // sl
    for cb in range(d // LANES):
        cols = slice(cb * LANES, (cb + 1) * LANES)

        def body(rc, carry, cols=cols):
            r0 = pl.multiple_of(rc * CONV_RC, CONV_RC)
            accs = [None] * nv
            for k in range(taps):
                s, a = (k + off0) % sl, (k + off0) // sl
                wk = wb_ref[k * sl:(k + 1) * sl, cols]
                for j in range(nv):
                    rows = pl.ds(r0 + sl * (a + j), sl)
                    win = ue_ref[rows, cols] if s == 0 else sh_ref[s - 1, rows, cols]
                    term = wk * win
                    accs[j] = term if accs[j] is None else accs[j] + term
            for j in range(nv):
                conv_ref[pl.ds(r0 + sl * j, sl), cols] = accs[j]
            return carry

        lax.fori_loop(0, tm // CONV_RC, body, 0)
    c = conv_ref[...] + dwb_ref[...]
    mu = jnp.mean(c, axis=-1, keepdims=True)
    cc = c - mu
    var = jnp.mean(cc * cc, axis=-1, keepdims=True)
    z = cc * lax.rsqrt(var + EPS) * lng_ref[...] + lnb_ref[...]
    z = (z * _sigmoid(z)).astype(BF16)
    y = _dot(z, w2_ref[...]) + b2_ref[...]
    _residual_norm_route(x_ref[...], y, g1_ref[...], n2g_ref[...], sh2_ref[...], sc2_ref[...],
                         rwt_ref[...], rb_ref[...], xo_ref, ho_ref, ri_ref, rf_ref, cnt_ref)


def _conf(u, dww, dwb, lng, lnb, w2, b2, x2, mod, n2g, rwt, rb, tpb):
    n, d = x2.shape
    tm = TMD
    hb = tm // HALO
    nh = n // HALO
    taps = dww.shape[0]
    assert (taps - 1) // 2 <= HALO - 1 and taps // 2 <= HALO
    one = pl.BlockSpec((1, d), lambda t: (0, 0))
    tail_in, out_specs, out_shape = _tail_specs(n, d, tm, mod, 1, tpb)
    in_specs = [pl.BlockSpec((tm, d), lambda t: (t, 0)),
                pl.BlockSpec((HALO, d), lambda t: (jnp.maximum(t * hb - 1, 0), 0)),
                pl.BlockSpec((HALO, d), lambda t: (jnp.minimum((t + 1) * hb, nh - 1), 0)),
                pl.BlockSpec(dww.shape, lambda t: (0, 0)), one, one, one,
                pl.BlockSpec(w2.shape, lambda t: (0, 0)), one] + tail_in
    ext = tm + 2 * HALO - SUBLANES_F32
    return pl.pallas_call(
        functools.partial(_conf_kernel, tpb),
        grid=(n // tm,),
        in_specs=in_specs, out_specs=out_specs, out_shape=out_shape,
        scratch_shapes=[pltpu.VMEM((tm + 2 * HALO, d), F32),
                        pltpu.VMEM((SUBLANES_F32 - 1, ext, d), F32),
                        pltpu.VMEM((tm, d), F32),
                        pltpu.VMEM((taps * SUBLANES_F32, d), F32)],
        compiler_params=_cparams(("arbitrary",)),
        name="conf_conv_route",
    )(u, u, u, dww, dwb, lng, lnb, w2, b2, x2, mod.arr, n2g, mod.arr, mod.arr, rwt, rb)


def _sorted_rows(n):
    t_n = n // TMD
    rows = 2 * n + (CHUNK - 1) * t_n * N_EXPERTS + N_EXPERTS * (TME - CHUNK)
    return (rows + TME - 1) // TME * TME


def _moe_plan(cnt, n):
    t_n, e_n = cnt.shape
    rt = _sorted_rows(n)
    seg = (cnt + CHUNK - 1) // CHUNK * CHUNK
    lo = jnp.cumsum(seg, axis=1) - seg
    etot = jnp.sum(seg, axis=0)
    epad = (etot + TME - 1) // TME * TME
    eend = jnp.cumsum(epad)
    estart = eend - epad
    go = estart[None, :] + jnp.cumsum(seg, axis=0) - seg
    nchunk = jnp.sum(seg, axis=1) // CHUNK
    e_ids = jnp.arange(e_n, dtype=I32)

    def by_owner(count, slots):
        end = jnp.cumsum(count, axis=1)
        idx = jnp.arange(slots, dtype=I32)
        own = jnp.minimum(jnp.sum((end[:, None, :] <= idx[None, :, None]).astype(I32), axis=-1), e_n - 1)
        pick = lambda a: jnp.sum(jnp.where(own[:, :, None] == e_ids, a[:, None, :], 0), axis=-1)
        return pick, idx[None, :] - pick(end - count)

    segc, lo_c, go_c = seg // CHUNK, lo // CHUNK, go // CHUNK
    pick2, j2 = by_owner(segc // 2, PAIRS_MAX)
    pick1, _ = by_owner(segc % 2, N_EXPERTS)
    nglob = rt // CHUNK
    clist = jnp.concatenate([
        jnp.clip(pick2(lo_c) + 2 * j2, 0, CMAX - 2), jnp.clip(pick2(go_c) + 2 * j2, 0, nglob - 2),
        jnp.clip(pick1(lo_c + segc - 1), 0, CMAX - 1), jnp.clip(pick1(go_c + segc - 1), 0, nglob - 1)],
        axis=1)
    counts = jnp.stack([nchunk, jnp.sum(segc // 2, axis=1), jnp.sum(segc % 2, axis=1)], axis=1)
    npad = (epad - etot) // CHUNK
    pend = jnp.cumsum(npad)
    kk = jnp.arange(e_n * (TME // CHUNK - 1), dtype=I32)
    pown = jnp.minimum(jnp.sum((pend[None, :] <= kk[:, None]).astype(I32), axis=-1), e_n - 1)
    pbase = (estart + etot) // CHUNK - (pend - npad)
    padmap = kk + jnp.sum(jnp.where(pown[:, None] == e_ids, pbase[None, :], 0), axis=-1)
    padmap = jnp.clip(padmap, 0, rt // CHUNK - 1)
    tile_row = jnp.arange(rt // TME, dtype=I32) * TME
    texp = jnp.minimum(jnp.sum((eend[None, :] <= tile_row[:, None]).astype(I32), axis=-1), e_n - 1)
    as_i32 = lambda a: a.astype(I32)
    return dict(lo=as_i32(lo.reshape(-1)), nchunk=as_i32(counts.reshape(-1)), cmap=as_i32(clist.reshape(-1)),
                padmap=as_i32(padmap), npadtot=as_i32(pend[-1:]), texp=as_i32(texp),
                nact=as_i32(eend[-1:] // TME))


def _chunk_copy(src_ref, dst_ref, sem):
    return pltpu.make_async_copy(src_ref, dst_ref, sem)


def _start_tile_copies(counts_ref, clist_ref, tt, make):
    base = tt * COPY_LIST

    def pair(i, c):
        make(clist_ref[base + i], clist_ref[base + PAIRS_MAX + i], 2).start()
        return c

    def single(i, c):
        make(clist_ref[base + 2 * PAIRS_MAX + i], clist_ref[base + 2 * PAIRS_MAX + N_EXPERTS + i], 1).start()
        return c

    lax.fori_loop(0, counts_ref[3 * tt + 1], pair, 0)
    lax.fori_loop(0, counts_ref[3 * tt + 2], single, 0)


def _wait_chunks(n_chunks, desc):
    for bit in range(CMAX.bit_length()):
        @pl.when(((n_chunks >> bit) & 1) == 1)
        def _(bit=bit):
            desc(CHUNK << bit).wait()


def _dispatch_kernel(nchunk_ref, cmap_ref, npad_ref, padmap_ref, nact_ref, lo_ref,
                     h_ref, ri_ref, xs_ref, lp_ref, xc_ref, z_ref, sem, zsem):
    t = pl.program_id(0)
    nt = pl.num_programs(0)
    slot = t % 2
    ntail = xs_ref.shape[0] // TME - nact_ref[0]

    def pad_copy(k):
        g = pl.multiple_of(padmap_ref[k] * CHUNK, CHUNK)
        return _chunk_copy(z_ref.at[0:CHUNK, :], xs_ref.at[pl.ds(g, CHUNK), :], zsem.at[0])

    def tail_copy(k):
        g = pl.multiple_of((nact_ref[0] + k) * TME, TME)
        return _chunk_copy(z_ref, xs_ref.at[pl.ds(g, TME), :], zsem.at[0])

    @pl.when(t == 0)
    def _():
        z_ref[...] = jnp.zeros(z_ref.shape, z_ref.dtype)
        lax.fori_loop(0, npad_ref[0], lambda k, c: (pad_copy(k).start(), c)[1], 0)
        lax.fori_loop(0, ntail, lambda k, c: (tail_copy(k).start(), c)[1], 0)

    e0, e1 = ri_ref[0:1, :], ri_ref[1:2, :]
    b0 = jnp.zeros(e0.shape, I32)
    b1 = jnp.zeros(e0.shape, I32)
    for e in range(N_EXPERTS):
        lo_e = lo_ref[t * N_EXPERTS + e]
        b0 = jnp.where(e0 == e, lo_e, b0)
        b1 = jnp.where(e1 == e, lo_e, b1)
    lp0 = b0 + ri_ref[2:3, :]
    lp1 = b1 + ri_ref[3:4, :]
    lp_ref[...] = jnp.zeros(lp_ref.shape, I32)
    lp_ref[0:1, :] = lp0
    lp_ref[1:2, :] = lp1

    rows = lax.broadcasted_iota(I32, (RCOMP, TMD), 0)
    onehot = jnp.where((lp0 == rows) | (lp1 == rows), 1.0, 0.0).astype(BF16)
    xc_ref[slot] = _dot(onehot, h_ref[...]).astype(BF16)

    def out_copy(lc, gc, k):
        l = pl.multiple_of(lc * CHUNK, CHUNK)
        g = pl.multiple_of(gc * CHUNK, CHUNK)
        return _chunk_copy(xc_ref.at[slot, pl.ds(l, k * CHUNK), :], xs_ref.at[pl.ds(g, k * CHUNK), :],
                           sem.at[slot])

    _start_tile_copies(nchunk_ref, cmap_ref, t, out_copy)

    def out_desc(sl):
        return lambda rows: _chunk_copy(xc_ref.at[sl, 0:rows, :], xs_ref.at[0:rows, :], sem.at[sl])

    @pl.when(t > 0)
    def _():
        _wait_chunks(nchunk_ref[3 * (t - 1)], out_desc(1 - slot))

    @pl.when(t == 0)
    def _():
        lax.fori_loop(0, npad_ref[0], lambda k, c: (pad_copy(k).wait(), c)[1], 0)
        lax.fori_loop(0, ntail, lambda k, c: (tail_copy(k).wait(), c)[1], 0)

    @pl.when(t == nt - 1)
    def _():
        _wait_chunks(nchunk_ref[3 * t], out_desc(slot))


def _dispatch(h2, ri, plan):
    n, d = h2.shape
    t_n = n // TMD
    rt = _sorted_rows(n)
    grid_spec = pltpu.PrefetchScalarGridSpec(
        num_scalar_prefetch=6,
        grid=(t_n,),
        in_specs=[pl.BlockSpec((TMD, d), lambda t, *_: (t, 0)),
                  pl.BlockSpec((SUBLANES_F32, TMD), lambda t, *_: (0, t))],
        out_specs=[pl.BlockSpec(memory_space=pl.ANY),
                   pl.BlockSpec((None, SUBLANES_F32, TMD), lambda t, *_: (t, 0, 0))],
        scratch_shapes=[pltpu.VMEM((2, RCOMP, d), BF16),
                        pltpu.VMEM((TME, d), BF16),
                        pltpu.SemaphoreType.DMA((2,)),
                        pltpu.SemaphoreType.DMA((1,))],
    )
    return pl.pallas_call(
        _dispatch_kernel,
        grid_spec=grid_spec,
        out_shape=[jax.ShapeDtypeStruct((rt, d), BF16),
                   jax.ShapeDtypeStruct((t_n, SUBLANES_F32, TMD), I32)],
        compiler_params=_cparams(("arbitrary",)),
        name="moe_dispatch",
    )(plan["nchunk"], plan["cmap"], plan["npadtot"], plan["padmap"], plan["nact"], plan["lo"], h2, ri)


def _expert_kernel(texp_ref, nact_ref, x_ref, w1_ref, w3_ref, w2_ref, y_ref, w1b, w3b, w2b):
    i = pl.program_id(0)

    @pl.when(i < nact_ref[0])
    def _():
        prev = texp_ref[jnp.maximum(i - 1, 0)]

        @pl.when((i == 0) | (texp_ref[i] != prev))
        def _():
            w1b[...] = w1_ref[...].astype(BF16)
            w3b[...] = w3_ref[...].astype(BF16)
            w2b[...] = w2_ref[...].astype(BF16)

        x = x_ref[...]
        h = _dot(x, w1b[...])
        g = _dot(x, w3b[...])
        a = (h * _sigmoid(h) * g).astype(BF16)
        y_ref[...] = _dot(a, w2b[...]).astype(y_ref.dtype)


def _experts(xs, w1, w3, w2, layer, plan):
    rt, d = xs.shape
    f = w1.shape[-1]
    nt = rt // TME

    def row_map(i, texp, nact):
        return (jnp.clip(i, 0, jnp.maximum(nact[0] - 1, 0)), 0)

    def w_map(i, texp, nact):
        return (layer, texp[jnp.clip(i, 0, jnp.maximum(nact[0] - 1, 0))], 0, 0)

    grid_spec = pltpu.PrefetchScalarGridSpec(
        num_scalar_prefetch=2,
        grid=(nt,),
        in_specs=[pl.BlockSpec((TME, d), row_map),
                  pl.BlockSpec((None, None, d, f), w_map),
                  pl.BlockSpec((None, None, d, f), w_map),
                  pl.BlockSpec((None, None, f, d), w_map)],
        out_specs=pl.BlockSpec((TME, d), row_map),
        scratch_shapes=[pltpu.VMEM((d, f), BF16), pltpu.VMEM((d, f), BF16), pltpu.VMEM((f, d), BF16)],
    )
    return pl.pallas_call(
        _expert_kernel,
        grid_spec=grid_spec,
        out_shape=jax.ShapeDtypeStruct((rt, d), BF16),
        input_output_aliases={2: 0},
        compiler_params=_cparams(("arbitrary",)),
        name="moe_experts",
    )(plan["texp"], plan["nact"], xs, w1, w3, w2)


def _combine_core(nchunk_ref, cmap_ref, ys_ref, ci_ref, cf_ref, x_ref, g2_ref, yc_ref, sem):
    t = pl.program_id(0)
    nt = pl.num_programs(0)
    slot = t % 2

    def in_copy(sl):
        def make(lc, gc, k):
            l = pl.multiple_of(lc * CHUNK, CHUNK)
            g = pl.multiple_of(gc * CHUNK, CHUNK)
            return _chunk_copy(ys_ref.at[pl.ds(g, k * CHUNK), :], yc_ref.at[sl, pl.ds(l, k * CHUNK), :],
                               sem.at[sl])
        return make

    @pl.when(t == 0)
    def _():
        yc_ref[...] = jnp.zeros(yc_ref.shape, yc_ref.dtype)
        _start_tile_copies(nchunk_ref, cmap_ref, 0, in_copy(0))

    @pl.when(t + 1 < nt)
    def _():
        _start_tile_copies(nchunk_ref, cmap_ref, t + 1, in_copy(1 - slot))

    _wait_chunks(nchunk_ref[3 * t],
                 lambda rows: _chunk_copy(ys_ref.at[0:rows, :], yc_ref.at[slot, 0:rows, :], sem.at[slot]))

    ci = ci_ref[...]
    cf = cf_ref[...]
    cols = lax.broadcasted_iota(I32, (TMD, RCOMP), 1)
    w = jnp.where(ci[:, 0:1] == cols, cf[:, 0:1], 0.0) + jnp.where(ci[:, 1:2] == cols, cf[:, 1:2], 0.0)
    moe = _dot(w.astype(BF16), yc_ref[slot])
    return x_ref[...] + g2_ref[...] * moe


def _combine_final_kernel(nchunk_ref, cmap_ref, ys_ref, ci_ref, cf_ref, x_ref, g2_ref, fg_ref,
                          o_ref, yc_ref, sem):
    x1 = _combine_core(nchunk_ref, cmap_ref, ys_ref, ci_ref, cf_ref, x_ref, g2_ref, yc_ref, sem)
    ms = jnp.mean(x1 * x1, axis=-1, keepdims=True)
    o_ref[...] = x1 * lax.rsqrt(ms + EPS) * fg_ref[...]


def _combine_pw1_kernel(nchunk_ref, cmap_ref, ys_ref, ci_ref, cf_ref, x_ref, g2_ref,
                        n1g_ref, sh_ref, sc_ref, w_ref, b_ref, o_ref, u_ref, yc_ref, sem):
    x1 = _combine_core(nchunk_ref, cmap_ref, ys_ref, ci_ref, cf_ref, x_ref, g2_ref, yc_ref, sem)
    o_ref[...] = x1
    d = x1.shape[1]
    h = _norm_mod(x1, n1g_ref[...], sh_ref[...], sc_ref[...]).astype(BF16)
    a = _dot(h, w_ref[:, 0:d]) + b_ref[:, 0:d]
    g = _dot(h, w_ref[:, d:2 * d]) + b_ref[:, d:2 * d]
    u_ref[...] = (a * _sigmoid(g)).astype(BF16)


def _combine(ys, plan, lp, rf, x2, mod, layer, tpb, tail):
    n, d = x2.shape
    t_n = n // TMD
    col = pl.BlockSpec((TMD, SUBLANES_F32), lambda t, *_: (t, 0))
    one = pl.BlockSpec((1, d), lambda t, *_: (0, 0))
    tile = pl.BlockSpec((TMD, d), lambda t, *_: (t, 0))
    in_specs = [pl.BlockSpec(memory_space=pl.ANY), col, col, tile, mod.vec(layer, 5, tpb)]
    args = [plan["nchunk"], plan["cmap"], ys,
            jnp.transpose(lp, (0, 2, 1)).reshape(n, SUBLANES_F32), rf.T, x2, mod.arr]
    if tail[0] == "final":
        body, name = _combine_final_kernel, "moe_combine_final"
        in_specs += [one]
        args += [tail[1]]
        out_specs = tile
        out_shape = jax.ShapeDtypeStruct((n, d), F32)
    else:
        body, name = _combine_pw1_kernel, "moe_combine_pw1"
        _, n1g, w_bf, bias = tail
        in_specs += [one, mod.vec(layer + 1, 0, tpb), mod.vec(layer + 1, 1, tpb),
                     pl.BlockSpec(w_bf.shape, lambda t, *_: (0, 0)),
                     pl.BlockSpec(bias.shape, lambda t, *_: (0, 0))]
        args += [n1g, mod.arr, mod.arr, w_bf, bias]
        out_specs = [tile, tile]
        out_shape = [jax.ShapeDtypeStruct((n, d), F32), jax.ShapeDtypeStruct((n, d), BF16)]
    grid_spec = pltpu.PrefetchScalarGridSpec(
        num_scalar_prefetch=2,
        grid=(t_n,),
        in_specs=in_specs,
        out_specs=out_specs,
        scratch_shapes=[pltpu.VMEM((2, RCOMP, d), BF16), pltpu.SemaphoreType.DMA((2,))],
    )
    return pl.pallas_call(body, grid_spec=grid_spec, out_shape=out_shape,
                          compiler_params=_cparams(("arbitrary",)), name=name)(*args)


def _moe(x1, h2, ri, rf, cnt, w1, w3, w2, layer, mod, tpb, tail):
    n = x1.shape[0]
    plan = _moe_plan(cnt[:, :, 0], n)
    xs, lp = _dispatch(h2, ri, plan)
    ys = _experts(xs, w1, w3, w2, layer, plan)
    return _combine(ys, plan, lp, rf, x1, mod, layer, tpb, tail)


def kernel(x, c, ctx, c_ctx, norm1_g, norm2_g, w_mod, b_mod, na_w_in, na_rpb, sc_conv_w, ab_w_out,
           cf_pw1_w, cf_pw1_b, cf_dw_w, cf_dw_b, cf_ln_g, cf_ln_b, cf_pw2_w, cf_pw2_b,
           router_w, router_b, moe_w1, moe_w3, moe_w2, final_g):
    b, s, d = x.shape
    l = ctx.shape[1]
    n = b * s
    depth = w_mod.shape[0]
    assert depth == 2, "layer pattern (attention layer, Conformer layer) is written out for depth 2"
    assert s % TM_PROJ == 0 and s % TMD == 0
    rows = s // GRID_W
    assert rows % ATT_ROWS == 0 and rows >= ATT_WIN and s % ATT_KBLK == 0
    tpb = s // TM_PROJ
    tpb_d = s // TMD

    mr = (b + 1 + SUBLANES_F32 - 1) // SUBLANES_F32 * SUBLANES_F32
    c_all = jnp.concatenate([c, c_ctx[None, :], jnp.zeros((mr - b - 1, d), F32)], axis=0)
    mod = _Mod(_modulation(c_all, w_mod, b_mod), d)

    x2 = x.reshape(n, d)
    rwt = router_w.T
    rb = router_b.reshape(N_EXPERTS, 1)
    fg = final_g.reshape(1, d)

    w_in = na_w_in[0].astype(BF16)
    n1g = norm1_g[0].reshape(1, d)
    wkt = w_in[:, NA_WIDTH:2 * NA_WIDTH].T
    q, kt, v, gb, u = _inproj(x2, n1g, mod, w_in, wkt, tpb)
    kct, vc = _ctxproj(ctx.reshape(b * l, d), n1g, mod, b, wkt, w_in[:, 2 * NA_WIDTH:3 * NA_WIDTH], l)
    tab = _attn_bias_tables(na_rpb[0], rows)
    a = _attention(q, kt, v, kct, vc, tab, b, s, l)
    w_out = ab_w_out[0].astype(BF16)
    x1, h2, ri, rf, cnt = _outproj(a, gb, u, sc_conv_w[0], w_out[:NA_WIDTH], w_out[NA_WIDTH:], x2,
                                   mod, norm2_g[0].reshape(1, d), rwt, rb, tpb_d)
    x2, u = _moe(x1, h2, ri, rf, cnt, moe_w1, moe_w3, moe_w2, 0, mod, tpb_d,
                 ("pw1", norm1_g[1].reshape(1, d), cf_pw1_w[0].astype(BF16), cf_pw1_b[0].reshape(1, 2 * d)))
    x1, h2, ri, rf, cnt = _conf(u, cf_dw_w[0], cf_dw_b[0].reshape(1, d), cf_ln_g[0].reshape(1, d),
                                cf_ln_b[0].reshape(1, d), cf_pw2_w[0].astype(BF16),
                                cf_pw2_b[0].reshape(1, d), x2, mod, norm2_g[1].reshape(1, d),
                                rwt, rb, tpb_d)
    out = _moe(x1, h2, ri, rf, cnt, moe_w1, moe_w3, moe_w2, 1, mod, tpb_d, ("final", fg))
    return out.reshape(b, s, d)
```

```python
import functools

import numpy as np
import jax
import jax.numpy as jnp
from jax import lax
from jax.experimental import pallas as pl
from jax.experimental.pallas import tpu as pltpu

F32 = jnp.float32
BF16 = jnp.bfloat16
I32 = jnp.int32

GRID_W = 64
NA_HEADS = 8
NA_HEAD_DIM = 64
NA_WIDTH = NA_HEADS * NA_HEAD_DIM
NA_KR = 8
NA_KC = 16
N_EXPERTS = 16
N_GROUPS = 4
EXPERTS_PER_GROUP = N_EXPERTS // N_GROUPS
N_MOD = 6
EPS = 1e-6
NEG = -1e30
LOG2E = 1.4426950408889634

LANES = 128
SUBLANES_F32 = 8
SUBLANES_BF16 = 16

TM_PROJ = 1024
ATT_ROWS = 8
ATT_WIN = 2 * ATT_ROWS
ATT_KBLK = 256
ATT_HEADS = 8
TMD = 512
CHUNK = SUBLANES_BF16
CMAX = (2 * TMD + (N_EXPERTS - 1) * CHUNK + CHUNK - 1) // CHUNK + 1
RCOMP = CMAX * CHUNK
PAIRS_MAX = CMAX // 2
COPY_LIST = 2 * PAIRS_MAX + 2 * N_EXPERTS
TME = 1024
CONV_RC = 128
HALO = SUBLANES_BF16
VMEM_LIMIT = 52 * 1024 * 1024


def _cparams(sem):
    return pltpu.CompilerParams(dimension_semantics=sem, vmem_limit_bytes=VMEM_LIMIT)


def _dot(a, b):
    return jnp.dot(a, b, preferred_element_type=F32)


def _dot_nt(a, b, precision=None):
    return lax.dot_general(a, b, (((1,), (1,)), ((), ())), precision=precision,
                           preferred_element_type=F32)


def _sigmoid(x):
    return 1.0 / (1.0 + jnp.exp(-x))


def _norm_mod(x, g, sh, sc):
    ms = jnp.mean(x * x, axis=-1, keepdims=True)
    y = x * lax.rsqrt(ms + EPS) * g
    return y * (1.0 + sc) + sh


def _split_bf16(a):
    hi = a.astype(BF16)
    lo = (a - hi.astype(F32)).astype(BF16)
    return hi, lo


def _mod_kernel(c_ref, w_ref, b_ref, o_ref):
    c = c_ref[...]
    mr = c.shape[0]
    hi, lo = _split_bf16(c * _sigmoid(c))
    w_hi, w_lo = _split_bf16(w_ref[...])
    r = _dot(jnp.concatenate([hi, lo], axis=0), w_hi)
    o_ref[...] = r[:mr] + r[mr:] + _dot(hi, w_lo) + b_ref[...]


def _modulation(c_all, w_mod, b_mod):
    depth, d, n6 = w_mod.shape
    mr = c_all.shape[0]
    tn = 2048
    return pl.pallas_call(
        _mod_kernel,
        grid=(depth, n6 // tn),
        in_specs=[pl.BlockSpec((mr, d), lambda i, j: (0, 0)),
                  pl.BlockSpec((None, d, tn), lambda i, j: (i, 0, j)),
                  pl.BlockSpec((None, 1, tn), lambda i, j: (i, 0, j))],
        out_specs=pl.BlockSpec((None, mr, tn), lambda i, j: (i, 0, j)),
        out_shape=jax.ShapeDtypeStruct((depth, mr, n6), F32),
        compiler_params=_cparams(("arbitrary", "arbitrary")),
        name="modulation",
    )(c_all, w_mod, b_mod.reshape(depth, 1, n6))


class _Mod:
    def __init__(self, mod, d):
        self.depth, self.mr, _ = mod.shape
        self.d = d
        self.arr = mod.reshape(self.depth * self.mr * N_MOD, 1, d)

    def vec(self, layer, j, tpb):
        mr = self.mr
        return pl.BlockSpec((None, 1, self.d),
                            lambda t, *_: ((layer * mr + t // tpb) * N_MOD + j, 0, 0))

    def row(self, layer, j, r):
        mr = self.mr
        return pl.BlockSpec((None, 1, self.d), lambda t, *_: ((layer * mr + r) * N_MOD + j, 0, 0))


def _inproj_kernel(x_ref, g_ref, sh_ref, sc_ref, w_ref, wkt_ref, q_ref, kt_ref, v_ref, gb_ref, u_ref):
    h = _norm_mod(x_ref[...], g_ref[...], sh_ref[...], sc_ref[...]).astype(BF16)
    w = NA_WIDTH
    q_ref[...] = (_dot(h, w_ref[:, 0:w]) * (NA_HEAD_DIM ** -0.5 * LOG2E)).astype(BF16)
    kt_ref[...] = _dot_nt(wkt_ref[...], h).astype(BF16)
    v_ref[...] = _dot(h, w_ref[:, 2 * w:3 * w]).astype(BF16)
    gb_ref[...] = _dot(h, w_ref[:, 3 * w:4 * w]).astype(BF16)
    u_ref[...] = (_dot(h, w_ref[:, 4 * w:5 * w]) * _dot(h, w_ref[:, 5 * w:6 * w])).astype(BF16)


def _inproj(x2, g, mod, w_bf, wkt_bf, tpb):
    n, d = x2.shape
    tm = TM_PROJ
    out = jax.ShapeDtypeStruct((n, NA_WIDTH), BF16)
    ospec = pl.BlockSpec((tm, NA_WIDTH), lambda t: (t, 0))
    out_t = jax.ShapeDtypeStruct((NA_WIDTH, n), BF16)
    ospec_t = pl.BlockSpec((NA_WIDTH, tm), lambda t: (0, t))
    return pl.pallas_call(
        _inproj_kernel,
        grid=(n // tm,),
        in_specs=[pl.BlockSpec((tm, d), lambda t: (t, 0)),
                  pl.BlockSpec((1, d), lambda t: (0, 0)), mod.vec(0, 0, tpb), mod.vec(0, 1, tpb),
                  pl.BlockSpec(w_bf.shape, lambda t: (0, 0)),
                  pl.BlockSpec(wkt_bf.shape, lambda t: (0, 0))],
        out_specs=[ospec, ospec_t, ospec, ospec, ospec],
        out_shape=[out, out_t, out, out, out],
        compiler_params=_cparams(("arbitrary",)),
        name="inproj",
    )(x2, g, mod.arr, mod.arr, w_bf, wkt_bf)


def _ctxproj_kernel(x_ref, g_ref, sh_ref, sc_ref, wkt_ref, wv_ref, kt_ref, v_ref):
    h = _norm_mod(x_ref[...], g_ref[...], sh_ref[...], sc_ref[...]).astype(BF16)
    kt_ref[...] = _dot_nt(wkt_ref[...], h).astype(BF16)
    v_ref[...] = _dot(h, wv_ref[...]).astype(BF16)


def _ctxproj(ctx2, g, mod, ctx_row, wkt_bf, wv_bf, l):
    n, d = ctx2.shape
    return pl.pallas_call(
        _ctxproj_kernel,
        grid=(n // l,),
        in_specs=[pl.BlockSpec((l, d), lambda t: (t, 0)), pl.BlockSpec((1, d), lambda t: (0, 0)),
                  mod.row(0, 0, ctx_row), mod.row(0, 1, ctx_row),
                  pl.BlockSpec(wkt_bf.shape, lambda t: (0, 0)),
                  pl.BlockSpec(wv_bf.shape, lambda t: (0, 0))],
        out_specs=[pl.BlockSpec((NA_WIDTH, l), lambda t: (0, t)),
                   pl.BlockSpec((l, NA_WIDTH), lambda t: (t, 0))],
        out_shape=[jax.ShapeDtypeStruct((NA_WIDTH, n), BF16), jax.ShapeDtypeStruct((n, NA_WIDTH), BF16)],
        compiler_params=_cparams(("arbitrary",)),
        name="ctxproj",
    )(ctx2, g, mod.arr, mod.arr, wkt_bf, wv_bf)


def _bias_row_index(rows):
    nb = rows // ATT_ROWS
    n_dr = 2 * NA_KR - 1
    dr_idx = np.full((3, ATT_ROWS, ATT_WIN), n_dr, np.int32)
    for ty, j in enumerate((0, 1, nb - 1)):
        ws = ATT_ROWS * j - NA_KR // 2
        for qr in range(ATT_ROWS):
            r = ATT_ROWS * j + qr
            rs = int(np.clip(r - NA_KR // 2, 0, rows - NA_KR))
            for kr in range(ATT_WIN):
                key_row = ws + kr
                if rs <= key_row < rs + NA_KR:
                    dr_idx[ty, qr, kr] = key_row - r + (NA_KR - 1)
    kr_blk = ATT_KBLK // GRID_W
    assert (dr_idx[:, :ATT_ROWS // 2, ATT_WIN - kr_blk:] == n_dr).all()
    assert (dr_idx[:, ATT_ROWS // 2:, :kr_blk] == n_dr).all()
    return dr_idx


def _bias_kernel(dr_idx, rpb_ref, out_ref, t_ref):
    h = pl.program_id(0)
    n_dr = 2 * NA_KR - 1
    n_dc = 2 * NA_KC - 1
    per_vreg = LANES // GRID_W
    assert per_vreg == 2 and ATT_WIN % per_vreg == 0
    qc = lax.broadcasted_iota(I32, (GRID_W, LANES), 0)
    lane = lax.broadcasted_iota(I32, (GRID_W, LANES), 1)
    kc = jnp.bitwise_and(lane, GRID_W - 1)
    cs = jnp.clip(qc - NA_KC // 2, 0, GRID_W - NA_KC)
    valid = (kc >= cs) & (kc < cs + NA_KC)
    dc = jnp.where(valid, kc - qc + (NA_KC - 1), -1)
    for dr in range(n_dr):
        slab = jnp.full((GRID_W, LANES), NEG, F32)
        for m in range(n_dc):
            slab = jnp.where(dc == m, rpb_ref[(h * n_dr + dr) * n_dc + m] * LOG2E, slab)
        t_ref[dr] = slab
    t_ref[n_dr] = jnp.full((GRID_W, LANES), NEG, F32)
    first = lane < GRID_W
    for ty in range(3):
        for qr in range(ATT_ROWS):
            for kp in range(ATT_WIN // per_vreg):
                d0, d1 = int(dr_idx[ty, qr, 2 * kp]), int(dr_idx[ty, qr, 2 * kp + 1])
                pair = t_ref[d0] if d0 == d1 else jnp.where(first, t_ref[d0], t_ref[d1])
                out_ref[ty, qr * GRID_W:(qr + 1) * GRID_W, kp * LANES:(kp + 1) * LANES] = pair


def _attn_bias_tables(rpb, rows):
    h = rpb.shape[0]
    dr_idx = _bias_row_index(rows)
    qn, kn = ATT_ROWS * GRID_W, ATT_WIN * GRID_W
    return pl.pallas_call(
        functools.partial(_bias_kernel, dr_idx),
        grid=(h,),
        in_specs=[pl.BlockSpec(memory_space=pltpu.SMEM)],
        out_specs=pl.BlockSpec((None, 3, qn, kn), lambda i: (i, 0, 0, 0)),
        out_shape=jax.ShapeDtypeStruct((h, 3, qn, kn), F32),
        scratch_shapes=[pltpu.VMEM((2 * NA_KR, GRID_W, LANES), F32)],
        compiler_params=_cparams(("arbitrary",)),
        name="attn_bias_table",
    )(rpb.astype(F32).reshape(-1))


def _attn_kernel(q_ref, k0_ref, k1_ref, k2_ref, k3_ref, kc_ref,
                 v0_ref, v1_ref, v2_ref, v3_ref, vc_ref, tab_ref, o_ref):
    lane = lax.broadcasted_iota(I32, (1, LANES), 1)
    kt_refs = [k0_ref, k1_ref, k2_ref, k3_ref, kc_ref]
    v_refs = [v0_ref, v1_ref, v2_ref, v3_ref, vc_ref]
    kb = ATT_KBLK
    half = q_ref.shape[0] // 2
    hms = [jnp.where((lane >= hh * NA_HEAD_DIM) & (lane < (hh + 1) * NA_HEAD_DIM), 1.0, 0.0).astype(BF16)
           for hh in range(2)]
    hmf = [hm.astype(F32) for hm in hms]
    for pp in range(ATT_HEADS // 2):
        cols = slice(pp * LANES, (pp + 1) * LANES)
        kt = [r[cols, :] for r in kt_refs]
        vv = [r[:, cols] for r in v_refs]
        for r0, blocks in ((0, (0, 1, 2)), (half, (1, 2, 3))):
            q = q_ref[r0:r0 + half, cols]
            qs = jnp.concatenate([q * hms[0], q * hms[1]], axis=0)
            s = [_dot(qs, kt[i])
                 + jnp.concatenate([tab_ref[2 * pp, r0:r0 + half, i * kb:(i + 1) * kb],
                                    tab_ref[2 * pp + 1, r0:r0 + half, i * kb:(i + 1) * kb]], axis=0)
                 for i in blocks]
            s.append(_dot(qs, kt[4]))
            m = jnp.max(s[0], axis=-1, keepdims=True)
            for si in s[1:]:
                m = jnp.maximum(m, jnp.max(si, axis=-1, keepdims=True))
            p = [jnp.exp2(si - m) for si in s]
            l = jnp.sum(p[0], axis=-1, keepdims=True)
            for pi in p[1:]:
                l = l + jnp.sum(pi, axis=-1, keepdims=True)
            o = _dot(p[3].astype(BF16), vv[4])
            for n_, i in enumerate(blocks):
                o = o + _dot(p[n_].astype(BF16), vv[i])
            o = o / l
            acc = o[:half] * hmf[0] + o[half:] * hmf[1]
            o_ref[r0:r0 + half, cols] = acc.astype(o_ref.dtype)


def _attention(q, kt, v, kct, vc, tab, b, s, l):
    n = q.shape[0]
    rows = s // GRID_W
    nb = rows // ATT_ROWS
    qblk = ATT_ROWS * GRID_W
    kpb = s // ATT_KBLK
    hg = NA_HEADS // ATT_HEADS
    hw = ATT_HEADS * NA_HEAD_DIM

    kper = ATT_ROWS * GRID_W // ATT_KBLK
    koff = (NA_KR // 2) * GRID_W // ATT_KBLK

    def kblk(j, i):
        return jnp.clip(kper * j - koff + i, 0, kpb - 1)

    qspec = pl.BlockSpec((qblk, hw), lambda h, j, bb: (bb * nb + j, h))
    ktspecs = [pl.BlockSpec((hw, ATT_KBLK), lambda h, j, bb, i=i: (h, bb * kpb + kblk(j, i)))
               for i in range(4)]
    vspecs = [pl.BlockSpec((ATT_KBLK, hw), lambda h, j, bb, i=i: (bb * kpb + kblk(j, i), h))
              for i in range(4)]
    ctspec = pl.BlockSpec((hw, l), lambda h, j, bb: (h, bb))
    cspec = pl.BlockSpec((l, hw), lambda h, j, bb: (bb, h))
    tspec = pl.BlockSpec((ATT_HEADS, None, qblk, ATT_WIN * GRID_W),
                         lambda h, j, bb: (h, jnp.where(j == 0, 0, jnp.where(j == nb - 1, 2, 1)), 0, 0))
    return pl.pallas_call(
        _attn_kernel,
        grid=(hg, nb, b),
        in_specs=[qspec] + ktspecs + [ctspec] + vspecs + [cspec, tspec],
        out_specs=qspec,
        out_shape=jax.ShapeDtypeStruct((n, NA_WIDTH), BF16),
        compiler_params=_cparams(("arbitrary", "arbitrary", "arbitrary")),
        name="nbr_attention",
    )(q, kt, kt, kt, kt, kct, v, v, v, v, vc, tab)


def _residual_norm_route(x, y, g1, n2g, sh2, sc2, rwt, rb, xo_ref, ho_ref, ri_ref, rf_ref, cnt_ref):
    tm = x.shape[0]
    x1 = x + g1 * y
    xo_ref[...] = x1
    h2 = _norm_mod(x1, n2g, sh2, sc2)
    h2_hi, h2_lo = _split_bf16(h2)
    ho_ref[...] = h2_hi
    w_hi, w_lo = _split_bf16(rwt)
    lg = _dot_nt(jnp.concatenate([w_hi, w_lo], axis=0), h2_hi)
    logits = lg[:N_EXPERTS] + lg[N_EXPERTS:] + _dot_nt(w_hi, h2_lo)
    scores = _sigmoid(logits)
    sel = scores + rb
    r = [sel[e:e + 1, :] for e in range(N_EXPERTS)]
    sc = [scores[e:e + 1, :] for e in range(N_EXPERTS)]
    grp = []
    for g in range(N_GROUPS):
        a, b_, c, d = r[4 * g:4 * g + 4]
        hi1, lo1 = jnp.maximum(a, b_), jnp.minimum(a, b_)
        hi2, lo2 = jnp.maximum(c, d), jnp.minimum(c, d)
        m1 = jnp.maximum(hi1, hi2)
        m2 = jnp.maximum(jnp.minimum(hi1, hi2), jnp.maximum(lo1, lo2))
        grp.append(m1 + m2)
    best = grp[0]
    gi = jnp.zeros(best.shape, I32)
    for g in range(1, N_GROUPS):
        upd = grp[g] > best
        best = jnp.where(upd, grp[g], best)
        gi = jnp.where(upd, g, gi)
    chosen = []
    for e in range(N_EXPERTS):
        g = e // EXPERTS_PER_GROUP
        rank = jnp.zeros(best.shape, I32)
        for e2 in range(EXPERTS_PER_GROUP * g, EXPERTS_PER_GROUP * (g + 1)):
            if e2 == e:
                continue
            ahead = (r[e2] > r[e]) | ((r[e2] == r[e]) & (e2 < e))
            rank = rank + jnp.where(ahead, 1, 0)
        chosen.append((gi == g) & (rank < 2))
    wsel = [jnp.where(chosen[e], sc[e], 0.0) for e in range(N_EXPERTS)]
    denom = wsel[0]
    for e in range(1, N_EXPERTS):
        denom = denom + wsel[e]
    selm = jnp.concatenate([jnp.where(chosen[e], 1.0, 0.0) for e in range(N_EXPERTS)], axis=0)
    ii = lax.broadcasted_iota(I32, (tm, tm), 0)
    jj = lax.broadcasted_iota(I32, (tm, tm), 1)
    tri = jnp.where(ii < jj, 1.0, 0.0).astype(BF16)
    prefix = _dot(selm.astype(BF16), tri)
    cnt = jnp.sum(selm, axis=1, keepdims=True)
    cnt_ref[...] = jnp.broadcast_to(cnt, cnt_ref.shape).astype(I32)
    seen = jnp.zeros(best.shape, I32)
    e0 = jnp.full(best.shape, -1, I32)
    e1 = jnp.full(best.shape, -1, I32)
    r0 = jnp.full(best.shape, -4 * RCOMP, I32)
    r1 = jnp.full(best.shape, -4 * RCOMP, I32)
    g0 = jnp.zeros(best.shape, F32)
    g1_ = jnp.zeros(best.shape, F32)
    for e in range(N_EXPERTS):
        first = chosen[e] & (seen == 0)
        second = chosen[e] & (seen == 1)
        pe = prefix[e:e + 1, :].astype(I32)
        ge = wsel[e] / denom
        e0 = jnp.where(first, e, e0)
        e1 = jnp.where(second, e, e1)
        r0 = jnp.where(first, pe, r0)
        r1 = jnp.where(second, pe, r1)
        g0 = jnp.where(first, ge, g0)
        g1_ = jnp.where(second, ge, g1_)
        seen = seen + jnp.where(chosen[e], 1, 0)
    ri_ref[...] = jnp.zeros(ri_ref.shape, I32)
    rf_ref[...] = jnp.zeros(rf_ref.shape, F32)
    ri_ref[0:1, :] = e0
    ri_ref[1:2, :] = e1
    ri_ref[2:3, :] = r0
    ri_ref[3:4, :] = r1
    rf_ref[0:1, :] = g0
    rf_ref[1:2, :] = g1_


def _tail_specs(n, d, tm, mod, layer, tpb):
    assert tm == TMD
    one = pl.BlockSpec((1, d), lambda t: (0, 0))
    in_specs = [pl.BlockSpec((tm, d), lambda t: (t, 0)),
                mod.vec(layer, 2, tpb), one, mod.vec(layer, 3, tpb), mod.vec(layer, 4, tpb),
                pl.BlockSpec((N_EXPERTS, d), lambda t: (0, 0)),
                pl.BlockSpec((N_EXPERTS, 1), lambda t: (0, 0))]
    out_specs = [pl.BlockSpec((tm, d), lambda t: (t, 0)),
                 pl.BlockSpec((tm, d), lambda t: (t, 0)),
                 pl.BlockSpec((SUBLANES_F32, tm), lambda t: (0, t)),
                 pl.BlockSpec((SUBLANES_F32, tm), lambda t: (0, t)),
                 pl.BlockSpec((None, N_EXPERTS, LANES), lambda t: (t, 0, 0))]
    out_shape = [jax.ShapeDtypeStruct((n, d), F32), jax.ShapeDtypeStruct((n, d), BF16),
                 jax.ShapeDtypeStruct((SUBLANES_F32, n), I32), jax.ShapeDtypeStruct((SUBLANES_F32, n), F32),
                 jax.ShapeDtypeStruct((n // tm, N_EXPERTS, LANES), I32)]
    return in_specs, out_specs, out_shape


def _outproj_kernel(tpb, a_ref, gb_ref, u_ref, up_ref, un_ref, cw_ref, wa_ref, wb_ref,
                    x_ref, g1_ref, n2g_ref, sh2_ref, sc2_ref, rwt_ref, rb_ref,
                    xo_ref, ho_ref, ri_ref, rf_ref, cnt_ref):
    t = pl.program_id(0)
    tm = u_ref.shape[0]
    u = u_ref[...].astype(F32)
    keep_prev = jnp.where(t % tpb == 0, 0.0, 1.0)
    keep_next = jnp.where(t % tpb == tpb - 1, 0.0, 1.0)
    prev_row = up_ref[HALO - 1:HALO, :].astype(F32) * keep_prev
    next_row = un_ref[0:1, :].astype(F32) * keep_next
    row = lax.broadcasted_iota(I32, u.shape, 0)
    u_m1 = jnp.where(row == 0, prev_row, pltpu.roll(u, 1, 0))
    u_p1 = jnp.where(row == tm - 1, next_row, pltpu.roll(u, tm - 1, 0))
    conv = cw_ref[0:1, :] * u_m1 + cw_ref[1:2, :] * u + cw_ref[2:3, :] * u_p1
    bx = (gb_ref[...].astype(F32) * conv).astype(BF16)
    y = _dot(a_ref[...], wa_ref[...]) + _dot(bx, wb_ref[...])
    _residual_norm_route(x_ref[...], y, g1_ref[...], n2g_ref[...], sh2_ref[...], sc2_ref[...],
                         rwt_ref[...], rb_ref[...], xo_ref, ho_ref, ri_ref, rf_ref, cnt_ref)


def _outproj(a, gb, u, cw, wa, wb, x2, mod, n2g, rwt, rb, tpb):
    n, d = x2.shape
    tm = TMD
    w = NA_WIDTH
    hb = tm // HALO
    nh = n // HALO
    half = pl.BlockSpec((tm, w), lambda t: (t, 0))
    tail_in, out_specs, out_shape = _tail_specs(n, d, tm, mod, 0, tpb)
    in_specs = [half, half, half,
                pl.BlockSpec((HALO, w), lambda t: (jnp.maximum(t * hb - 1, 0), 0)),
                pl.BlockSpec((HALO, w), lambda t: (jnp.minimum((t + 1) * hb, nh - 1), 0)),
                pl.BlockSpec(cw.shape, lambda t: (0, 0)),
                pl.BlockSpec(wa.shape, lambda t: (0, 0)),
                pl.BlockSpec(wb.shape, lambda t: (0, 0))] + tail_in
    return pl.pallas_call(
        functools.partial(_outproj_kernel, tpb),
        grid=(n // tm,),
        in_specs=in_specs, out_specs=out_specs, out_shape=out_shape,
        compiler_params=_cparams(("arbitrary",)),
        name="outproj_route",
    )(a, gb, u, u, u, cw, wa, wb, x2, mod.arr, n2g, mod.arr, mod.arr, rwt, rb)


def _conf_kernel(tpb, u_ref, up_ref, un_ref, dww_ref, dwb_ref, lng_ref, lnb_ref, w2_ref, b2_ref,
                 x_ref, g1_ref, n2g_ref, sh2_ref, sc2_ref, rwt_ref, rb_ref,
                 xo_ref, ho_ref, ri_ref, rf_ref, cnt_ref, ue_ref, sh_ref, conv_ref, wb_ref):
    t = pl.program_id(0)
    tm, d = u_ref.shape
    taps = dww_ref.shape[0]
    sl = SUBLANES_F32

    ncb = d // LANES

    @pl.when(t == 0)
    def _():
        for cb in range(ncb):
            for k in range(taps):
                wb_ref[cb, k * sl:(k + 1) * sl, :] = jnp.broadcast_to(
                    dww_ref[k:k + 1, cb * LANES:(cb + 1) * LANES], (sl, LANES))

    keep_prev = jnp.where(t % tpb == 0, 0.0, 1.0)
    keep_next = jnp.where(t % tpb == tpb - 1, 0.0, 1.0)
    ext = tm + 2 * HALO - sl
    for cb in range(ncb):
        cols = slice(cb * LANES, (cb + 1) * LANES)
        ue_ref[cb, 0:HALO, :] = up_ref[:, cols].astype(F32) * keep_prev
        ue_ref[cb, HALO:HALO + tm, :] = u_ref[:, cols].astype(F32)
        ue_ref[cb, HALO + tm:HALO + tm + HALO, :] = un_ref[:, cols].astype(F32) * keep_next
        for s in range(1, sl):
            sh_ref[s - 1, cb, :, :] = ue_ref[cb, s:s + ext, :]
    off0 = HALO - (taps - 1) // 2
    nv = CONV_RC // sl
    for cb in range(ncb):

        def body(rc, carry, cb=cb):
            r0 = pl.multiple_of(rc * CONV_RC, CONV_RC)
            accs = [None] * nv
            for k in range(taps):
                s, a = (k + off0) % sl, (k + off0) // sl
                wk = wb_ref[cb, k * sl:(k + 1) * sl, :]
                for j in range(nv):
                    rows = pl.ds(r0 + sl * (a + j), sl)
                    win = ue_ref[cb, rows, :] if s == 0 else sh_ref[s - 1, cb, rows, :]
                    term = wk * win
                    accs[j] = term if accs[j] is None else accs[j] + term
            for j in range(nv):
                conv_ref[cb, pl.ds(r0 + sl * j, sl), :] = accs[j]
            return carry

        lax.fori_loop(0, tm // CONV_RC, body, 0)
    c = jnp.concatenate([conv_ref[cb] for cb in range(ncb)], axis=1) + dwb_ref[...]
    mu = jnp.mean(c, axis=-1, keepdims=True)
    cc = c - mu
    var = jnp.mean(cc * cc, axis=-1, keepdims=True)
    z = cc * lax.rsqrt(var + EPS) * lng_ref[...] + lnb_ref[...]
    z = (z * _sigmoid(z)).astype(BF16)
    y = _dot(z, w2_ref[...]) + b2_ref[...]
    _residual_norm_route(x_ref[...], y, g1_ref[...], n2g_ref[...], sh2_ref[...], sc2_ref[...],
                         rwt_ref[...], rb_ref[...], xo_ref, ho_ref, ri_ref, rf_ref, cnt_ref)


def _conf(u, dww, dwb, lng, lnb, w2, b2, x2, mod, n2g, rwt, rb, tpb):
    n, d = x2.shape
    tm = TMD
    hb = tm // HALO
    nh = n // HALO
    taps = dww.shape[0]
    assert (taps - 1) // 2 <= HALO - 1 and taps // 2 <= HALO
    one = pl.BlockSpec((1, d), lambda t: (0, 0))
    tail_in, out_specs, out_shape = _tail_specs(n, d, tm, mod, 1, tpb)
    in_specs = [pl.BlockSpec((tm, d), lambda t: (t, 0)),
                pl.BlockSpec((HALO, d), lambda t: (jnp.maximum(t * hb - 1, 0), 0)),
                pl.BlockSpec((HALO, d), lambda t: (jnp.minimum((t + 1) * hb, nh - 1), 0)),
                pl.BlockSpec(dww.shape, lambda t: (0, 0)), one, one, one,
                pl.BlockSpec(w2.shape, lambda t: (0, 0)), one] + tail_in
    ext = tm + 2 * HALO - SUBLANES_F32
    return pl.pallas_call(
        functools.partial(_conf_kernel, tpb),
        grid=(n // tm,),
        in_specs=in_specs, out_specs=out_specs, out_shape=out_shape,
        scratch_shapes=[pltpu.VMEM((d // LANES, tm + 2 * HALO, LANES), F32),
                        pltpu.VMEM((SUBLANES_F32 - 1, d // LANES, ext, LANES), F32),
                        pltpu.VMEM((d // LANES, tm, LANES), F32),
                        pltpu.VMEM((d // LANES, taps * SUBLANES_F32, LANES), F32)],
        compiler_params=_cparams(("arbitrary",)),
        name="conf_conv_route",
    )(u, u, u, dww, dwb, lng, lnb, w2, b2, x2, mod.arr, n2g, mod.arr, mod.arr, rwt, rb)


def _sorted_rows(n):
    t_n = n // TMD
    rows = 2 * n + (CHUNK - 1) * t_n * N_EXPERTS + N_EXPERTS * (TME - CHUNK)
    return (rows + TME - 1) // TME * TME


def _moe_plan(cnt, n):
    t_n, e_n = cnt.shape
    rt = _sorted_rows(n)
    seg = (cnt + CHUNK - 1) // CHUNK * CHUNK
    lo = jnp.cumsum(seg, axis=1) - seg
    etot = jnp.sum(seg, axis=0)
    epad = (etot + TME - 1) // TME * TME
    eend = jnp.cumsum(epad)
    estart = eend - epad
    go = estart[None, :] + jnp.cumsum(seg, axis=0) - seg
    nchunk = jnp.sum(seg, axis=1) // CHUNK
    e_ids = jnp.arange(e_n, dtype=I32)

    def by_owner(count, slots):
        end = jnp.cumsum(count, axis=1)
        idx = jnp.arange(slots, dtype=I32)
        own = jnp.minimum(jnp.sum((end[:, None, :] <= idx[None, :, None]).astype(I32), axis=-1), e_n - 1)
        pick = lambda a: jnp.sum(jnp.where(own[:, :, None] == e_ids, a[:, None, :], 0), axis=-1)
        return pick, idx[None, :] - pick(end - count)

    segc, lo_c, go_c = seg // CHUNK, lo // CHUNK, go // CHUNK
    pick2, j2 = by_owner(segc // 2, PAIRS_MAX)
    pick1, _ = by_owner(segc % 2, N_EXPERTS)
    nglob = rt // CHUNK
    clist = jnp.concatenate([
        jnp.clip(pick2(lo_c) + 2 * j2, 0, CMAX - 2), jnp.clip(pick2(go_c) + 2 * j2, 0, nglob - 2),
        jnp.clip(pick1(lo_c + segc - 1), 0, CMAX - 1), jnp.clip(pick1(go_c + segc - 1), 0, nglob - 1)],
        axis=1)
    counts = jnp.stack([nchunk, jnp.sum(segc // 2, axis=1), jnp.sum(segc % 2, axis=1)], axis=1)
    npad = (epad - etot) // CHUNK
    pend = jnp.cumsum(npad)
    kk = jnp.arange(e_n * (TME // CHUNK - 1), dtype=I32)
    pown = jnp.minimum(jnp.sum((pend[None, :] <= kk[:, None]).astype(I32), axis=-1), e_n - 1)
    pbase = (estart + etot) // CHUNK - (pend - npad)
    padmap = kk + jnp.sum(jnp.where(pown[:, None] == e_ids, pbase[None, :], 0), axis=-1)
    padmap = jnp.clip(padmap, 0, rt // CHUNK - 1)
    tile_row = jnp.arange(rt // TME, dtype=I32) * TME
    texp = jnp.minimum(jnp.sum((eend[None, :] <= tile_row[:, None]).astype(I32), axis=-1), e_n - 1)
    as_i32 = lambda a: a.astype(I32)
    return dict(lo=as_i32(lo.reshape(-1)), nchunk=as_i32(counts.reshape(-1)), cmap=as_i32(clist.reshape(-1)),
                padmap=as_i32(padmap), npadtot=as_i32(pend[-1:]), texp=as_i32(texp),
                nact=as_i32(eend[-1:] // TME))


def _chunk_copy(src_ref, dst_ref, sem):
    return pltpu.make_async_copy(src_ref, dst_ref, sem)


def _start_tile_copies(counts_ref, clist_ref, tt, make):
    base = tt * COPY_LIST

    def pair(i, c):
        make(clist_ref[base + i], clist_ref[base + PAIRS_MAX + i], 2).start()
        return c

    def single(i, c):
        make(clist_ref[base + 2 * PAIRS_MAX + i], clist_ref[base + 2 * PAIRS_MAX + N_EXPERTS + i], 1).start()
        return c

    lax.fori_loop(0, counts_ref[3 * tt + 1], pair, 0)
    lax.fori_loop(0, counts_ref[3 * tt + 2], single, 0)


def _wait_chunks(n_chunks, desc):
    for bit in range(CMAX.bit_length()):
        @pl.when(((n_chunks >> bit) & 1) == 1)
        def _(bit=bit):
            desc(CHUNK << bit).wait()


def _dispatch_kernel(nchunk_ref, cmap_ref, npad_ref, padmap_ref, nact_ref, lo_ref,
                     h_ref, ri_ref, xs_ref, lp_ref, xc_ref, z_ref, sem, zsem):
    t = pl.program_id(0)
    nt = pl.num_programs(0)
    slot = t % 2
    ntail = xs_ref.shape[0] // TME - nact_ref[0]

    def pad_copy(k):
        g = pl.multiple_of(padmap_ref[k] * CHUNK, CHUNK)
        return _chunk_copy(z_ref.at[0:CHUNK, :], xs_ref.at[pl.ds(g, CHUNK), :], zsem.at[0])

    def tail_copy(k):
        g = pl.multiple_of((nact_ref[0] + k) * TME, TME)
        return _chunk_copy(z_ref, xs_ref.at[pl.ds(g, TME), :], zsem.at[0])

    @pl.when(t == 0)
    def _():
        z_ref[...] = jnp.zeros(z_ref.shape, z_ref.dtype)
        lax.fori_loop(0, npad_ref[0], lambda k, c: (pad_copy(k).start(), c)[1], 0)
        lax.fori_loop(0, ntail, lambda k, c: (tail_copy(k).start(), c)[1], 0)

    e0, e1 = ri_ref[0:1, :], ri_ref[1:2, :]
    b0 = jnp.zeros(e0.shape, I32)
    b1 = jnp.zeros(e0.shape, I32)
    for e in range(N_EXPERTS):
        lo_e = lo_ref[t * N_EXPERTS + e]
        b0 = jnp.where(e0 == e, lo_e, b0)
        b1 = jnp.where(e1 == e, lo_e, b1)
    lp0 = b0 + ri_ref[2:3, :]
    lp1 = b1 + ri_ref[3:4, :]
    lp_ref[...] = jnp.zeros(lp_ref.shape, I32)
    lp_ref[0:1, :] = lp0
    lp_ref[1:2, :] = lp1

    rows = lax.broadcasted_iota(I32, (RCOMP, TMD), 0)
    onehot = jnp.where((lp0 == rows) | (lp1 == rows), 1.0, 0.0).astype(BF16)
    xc_ref[slot] = _dot(onehot, h_ref[...]).astype(BF16)

    def out_copy(lc, gc, k):
        l = pl.multiple_of(lc * CHUNK, CHUNK)
        g = pl.multiple_of(gc * CHUNK, CHUNK)
        return _chunk_copy(xc_ref.at[slot, pl.ds(l, k * CHUNK), :], xs_ref.at[pl.ds(g, k * CHUNK), :],
                           sem.at[slot])

    _start_tile_copies(nchunk_ref, cmap_ref, t, out_copy)

    def out_desc(sl):
        return lambda rows: _chunk_copy(xc_ref.at[sl, 0:rows, :], xs_ref.at[0:rows, :], sem.at[sl])

    @pl.when(t > 0)
    def _():
        _wait_chunks(nchunk_ref[3 * (t - 1)], out_desc(1 - slot))

    @pl.when(t == 0)
    def _():
        lax.fori_loop(0, npad_ref[0], lambda k, c: (pad_copy(k).wait(), c)[1], 0)
        lax.fori_loop(0, ntail, lambda k, c: (tail_copy(k).wait(), c)[1], 0)

    @pl.when(t == nt - 1)
    def _():
        _wait_chunks(nchunk_ref[3 * t], out_desc(slot))


def _dispatch(h2, ri, plan):
    n, d = h2.shape
    t_n = n // TMD
    rt = _sorted_rows(n)
    grid_spec = pltpu.PrefetchScalarGridSpec(
        num_scalar_prefetch=6,
        grid=(t_n,),
        in_specs=[pl.BlockSpec((TMD, d), lambda t, *_: (t, 0)),
                  pl.BlockSpec((SUBLANES_F32, TMD), lambda t, *_: (0, t))],
        out_specs=[pl.BlockSpec(memory_space=pl.ANY),
                   pl.BlockSpec((None, SUBLANES_F32, TMD), lambda t, *_: (t, 0, 0))],
        scratch_shapes=[pltpu.VMEM((2, RCOMP, d), BF16),
                        pltpu.VMEM((TME, d), BF16),
                        pltpu.SemaphoreType.DMA((2,)),
                        pltpu.SemaphoreType.DMA((1,))],
    )
    return pl.pallas_call(
        _dispatch_kernel,
        grid_spec=grid_spec,
        out_shape=[jax.ShapeDtypeStruct((rt, d), BF16),
                   jax.ShapeDtypeStruct((t_n, SUBLANES_F32, TMD), I32)],
        compiler_params=_cparams(("arbitrary",)),
        name="moe_dispatch",
    )(plan["nchunk"], plan["cmap"], plan["npadtot"], plan["padmap"], plan["nact"], plan["lo"], h2, ri)


def _expert_kernel(texp_ref, nact_ref, x_ref, w1_ref, w3_ref, w2_ref, y_ref, w1b, w3b, w2b):
    i = pl.program_id(0)

    @pl.when(i < nact_ref[0])
    def _():
        prev = texp_ref[jnp.maximum(i - 1, 0)]

        @pl.when((i == 0) | (texp_ref[i] != prev))
        def _():
            w1b[...] = w1_ref[...].astype(BF16)
            w3b[...] = w3_ref[...].astype(BF16)
            w2b[...] = w2_ref[...].astype(BF16)

        x = x_ref[...]
        h = _dot(x, w1b[...])
        g = _dot(x, w3b[...])
        a = (h * _sigmoid(h) * g).astype(BF16)
        y_ref[...] = _dot(a, w2b[...]).astype(y_ref.dtype)


def _experts(xs, w1, w3, w2, layer, plan):
    rt, d = xs.shape
    f = w1.shape[-1]
    nt = rt // TME

    def row_map(i, texp, nact):
        return (jnp.clip(i, 0, jnp.maximum(nact[0] - 1, 0)), 0)

    def w_map(i, texp, nact):
        return (layer, texp[jnp.clip(i, 0, jnp.maximum(nact[0] - 1, 0))], 0, 0)

    grid_spec = pltpu.PrefetchScalarGridSpec(
        num_scalar_prefetch=2,
        grid=(nt,),
        in_specs=[pl.BlockSpec((TME, d), row_map),
                  pl.BlockSpec((None, None, d, f), w_map),
                  pl.BlockSpec((None, None, d, f), w_map),
                  pl.BlockSpec((None, None, f, d), w_map)],
        out_specs=pl.BlockSpec((TME, d), row_map),
        scratch_shapes=[pltpu.VMEM((d, f), BF16), pltpu.VMEM((d, f), BF16), pltpu.VMEM((f, d), BF16)],
    )
    return pl.pallas_call(
        _expert_kernel,
        grid_spec=grid_spec,
        out_shape=jax.ShapeDtypeStruct((rt, d), BF16),
        input_output_aliases={2: 0},
        compiler_params=_cparams(("arbitrary",)),
        name="moe_experts",
    )(plan["texp"], plan["nact"], xs, w1, w3, w2)


def _combine_core(nchunk_ref, cmap_ref, ys_ref, ci_ref, cf_ref, x_ref, g2_ref, yc_ref, sem):
    t = pl.program_id(0)
    nt = pl.num_programs(0)
    slot = t % 2

    def in_copy(sl):
        def make(lc, gc, k):
            l = pl.multiple_of(lc * CHUNK, CHUNK)
            g = pl.multiple_of(gc * CHUNK, CHUNK)
            return _chunk_copy(ys_ref.at[pl.ds(g, k * CHUNK), :], yc_ref.at[sl, pl.ds(l, k * CHUNK), :],
                               sem.at[sl])
        return make

    @pl.when(t == 0)
    def _():
        yc_ref[...] = jnp.zeros(yc_ref.shape, yc_ref.dtype)
        _start_tile_copies(nchunk_ref, cmap_ref, 0, in_copy(0))

    @pl.when(t + 1 < nt)
    def _():
        _start_tile_copies(nchunk_ref, cmap_ref, t + 1, in_copy(1 - slot))

    _wait_chunks(nchunk_ref[3 * t],
                 lambda rows: _chunk_copy(ys_ref.at[0:rows, :], yc_ref.at[slot, 0:rows, :], sem.at[slot]))

    ci = ci_ref[...]
    cf = cf_ref[...]
    cols = lax.broadcasted_iota(I32, (TMD, RCOMP), 1)
    w = jnp.where(ci[:, 0:1] == cols, cf[:, 0:1], 0.0) + jnp.where(ci[:, 1:2] == cols, cf[:, 1:2], 0.0)
    moe = _dot(w.astype(BF16), yc_ref[slot])
    return x_ref[...] + g2_ref[...] * moe


def _combine_final_kernel(nchunk_ref, cmap_ref, ys_ref, ci_ref, cf_ref, x_ref, g2_ref, fg_ref,
                          o_ref, yc_ref, sem):
    x1 = _combine_core(nchunk_ref, cmap_ref, ys_ref, ci_ref, cf_ref, x_ref, g2_ref, yc_ref, sem)
    ms = jnp.mean(x1 * x1, axis=-1, keepdims=True)
    o_ref[...] = x1 * lax.rsqrt(ms + EPS) * fg_ref[...]


def _combine_pw1_kernel(nchunk_ref, cmap_ref, ys_ref, ci_ref, cf_ref, x_ref, g2_ref,
                        n1g_ref, sh_ref, sc_ref, w_ref, b_ref, o_ref, u_ref, yc_ref, sem):
    x1 = _combine_core(nchunk_ref, cmap_ref, ys_ref, ci_ref, cf_ref, x_ref, g2_ref, yc_ref, sem)
    o_ref[...] = x1
    d = x1.shape[1]
    h = _norm_mod(x1, n1g_ref[...], sh_ref[...], sc_ref[...]).astype(BF16)
    a = _dot(h, w_ref[:, 0:d]) + b_ref[:, 0:d]
    g = _dot(h, w_ref[:, d:2 * d]) + b_ref[:, d:2 * d]
    u_ref[...] = (a * _sigmoid(g)).astype(BF16)


def _combine(ys, plan, lp, rf, x2, mod, layer, tpb, tail):
    n, d = x2.shape
    t_n = n // TMD
    col = pl.BlockSpec((TMD, SUBLANES_F32), lambda t, *_: (t, 0))
    one = pl.BlockSpec((1, d), lambda t, *_: (0, 0))
    tile = pl.BlockSpec((TMD, d), lambda t, *_: (t, 0))
    in_specs = [pl.BlockSpec(memory_space=pl.ANY), col, col, tile, mod.vec(layer, 5, tpb)]
    args = [plan["nchunk"], plan["cmap"], ys,
            jnp.transpose(lp, (0, 2, 1)).reshape(n, SUBLANES_F32), rf.T, x2, mod.arr]
    if tail[0] == "final":
        body, name = _combine_final_kernel, "moe_combine_final"
        in_specs += [one]
        args += [tail[1]]
        out_specs = tile
        out_shape = jax.ShapeDtypeStruct((n, d), F32)
    else:
        body, name = _combine_pw1_kernel, "moe_combine_pw1"
        _, n1g, w_bf, bias = tail
        in_specs += [one, mod.vec(layer + 1, 0, tpb), mod.vec(layer + 1, 1, tpb),
                     pl.BlockSpec(w_bf.shape, lambda t, *_: (0, 0)),
                     pl.BlockSpec(bias.shape, lambda t, *_: (0, 0))]
        args += [n1g, mod.arr, mod.arr, w_bf, bias]
        out_specs = [tile, tile]
        out_shape = [jax.ShapeDtypeStruct((n, d), F32), jax.ShapeDtypeStruct((n, d), BF16)]
    grid_spec = pltpu.PrefetchScalarGridSpec(
        num_scalar_prefetch=2,
        grid=(t_n,),
        in_specs=in_specs,
        out_specs=out_specs,
        scratch_shapes=[pltpu.VMEM((2, RCOMP, d), BF16), pltpu.SemaphoreType.DMA((2,))],
    )
    return pl.pallas_call(body, grid_spec=grid_spec, out_shape=out_shape,
                          compiler_params=_cparams(("arbitrary",)), name=name)(*args)


def _moe(x1, h2, ri, rf, cnt, w1, w3, w2, layer, mod, tpb, tail):
    n = x1.shape[0]
    plan = _moe_plan(cnt[:, :, 0], n)
    xs, lp = _dispatch(h2, ri, plan)
    ys = _experts(xs, w1, w3, w2, layer, plan)
    return _combine(ys, plan, lp, rf, x1, mod, layer, tpb, tail)


def kernel(x, c, ctx, c_ctx, norm1_g, norm2_g, w_mod, b_mod, na_w_in, na_rpb, sc_conv_w, ab_w_out,
           cf_pw1_w, cf_pw1_b, cf_dw_w, cf_dw_b, cf_ln_g, cf_ln_b, cf_pw2_w, cf_pw2_b,
           router_w, router_b, moe_w1, moe_w3, moe_w2, final_g):
    b, s, d = x.shape
    l = ctx.shape[1]
    n = b * s
    depth = w_mod.shape[0]
    assert depth == 2, "layer pattern (attention layer, Conformer layer) is written out for depth 2"
    assert s % TM_PROJ == 0 and s % TMD == 0
    rows = s // GRID_W
    assert rows % ATT_ROWS == 0 and rows >= ATT_WIN and s % ATT_KBLK == 0
    tpb = s // TM_PROJ
    tpb_d = s // TMD

    mr = (b + 1 + SUBLANES_F32 - 1) // SUBLANES_F32 * SUBLANES_F32
    c_all = jnp.concatenate([c, c_ctx[None, :], jnp.zeros((mr - b - 1, d), F32)], axis=0)
    mod = _Mod(_modulation(c_all, w_mod, b_mod), d)

    x2 = x.reshape(n, d)
    rwt = router_w.T
    rb = router_b.reshape(N_EXPERTS, 1)
    fg = final_g.reshape(1, d)

    w_in = na_w_in[0].astype(BF16)
    n1g = norm1_g[0].reshape(1, d)
    wkt = w_in[:, NA_WIDTH:2 * NA_WIDTH].T
    q, kt, v, gb, u = _inproj(x2, n1g, mod, w_in, wkt, tpb)
    kct, vc = _ctxproj(ctx.reshape(b * l, d), n1g, mod, b, wkt, w_in[:, 2 * NA_WIDTH:3 * NA_WIDTH], l)
    tab = _attn_bias_tables(na_rpb[0], rows)
    a = _attention(q, kt, v, kct, vc, tab, b, s, l)
    w_out = ab_w_out[0].astype(BF16)
    x1, h2, ri, rf, cnt = _outproj(a, gb, u, sc_conv_w[0], w_out[:NA_WIDTH], w_out[NA_WIDTH:], x2,
                                   mod, norm2_g[0].reshape(1, d), rwt, rb, tpb_d)
    x2, u = _moe(x1, h2, ri, rf, cnt, moe_w1, moe_w3, moe_w2, 0, mod, tpb_d,
                 ("pw1", norm1_g[1].reshape(1, d), cf_pw1_w[0].astype(BF16), cf_pw1_b[0].reshape(1, 2 * d)))
    x1, h2, ri, rf, cnt = _conf(u, cf_dw_w[0], cf_dw_b[0].reshape(1, d), cf_ln_g[0].reshape(1, d),
                                cf_ln_b[0].reshape(1, d), cf_pw2_w[0].astype(BF16),
                                cf_pw2_b[0].reshape(1, d), x2, mod, norm2_g[1].reshape(1, d),
                                rwt, rb, tpb_d)
    out = _moe(x1, h2, ri, rf, cnt, moe_w1, moe_w3, moe_w2, 1, mod, tpb_d, ("final", fg))
    return out.reshape(b, s, d)
```
